```python
import jax, jax.numpy as jnp
from jax import lax
import numpy as np

D_MODEL = 2048
BATCH = 2
SEQ = 8192
DEPTH = 4

HEAD_DIM = 128
N_HEADS_ATTN = 8
ATTN_WIDTH = N_HEADS_ATTN * HEAD_DIM
DILATED_PATTERNS = ((128, 1), (512, 4), (2048, 16))
ATTN_BLOCK = 128
CONV_WIDTH = D_MODEL - ATTN_WIDTH
CONV_KERNEL = 31
IN_PROJ_WIDTH = 3 * ATTN_WIDTH + 2 * CONV_WIDTH
POOL_WINDOWS = (2, 4, 8, 16)
POOL_GROUP = D_MODEL // len(POOL_WINDOWS)
N_GROUPS = 4
EXPERTS_PER_GROUP = 8
N_EXPERTS = N_GROUPS * EXPERTS_PER_GROUP
EXPERT_HIDDEN = 512
TOP_K_INNER = 2
ROPE_THETA = 10000.0
EPS = 1e-6
N_EVEN = (DEPTH + 1) // 2
N_ODD = DEPTH // 2

kernel_name = "hybrid_dilated_conv_pool_hmoe"


def rms_norm(x, g):
    xf = x.astype(jnp.float32)
    y = xf * lax.rsqrt(jnp.mean(xf * xf, axis=-1, keepdims=True) + EPS)
    return (y * g.astype(jnp.float32)).astype(x.dtype)


def rope(x, positions):
    half = HEAD_DIM // 2
    inv_freq = jnp.float32(ROPE_THETA) ** (-jnp.arange(half, dtype=jnp.float32) / half)
    ang = positions.astype(jnp.float32)[..., None] * inv_freq
    cos = jnp.cos(ang)[:, :, None, :]
    sin = jnp.sin(ang)[:, :, None, :]
    xf = x.astype(jnp.float32)
    x1, x2 = xf[..., :half], xf[..., half:]
    return jnp.concatenate([x1 * cos - x2 * sin, x1 * sin + x2 * cos], axis=-1).astype(x.dtype)


def dilated_branch(q, k, v, window, dilation):
    b, s, h, dh = q.shape
    n_back = window // dilation
    unit = dilation * ATTN_BLOCK
    s_pad = -(-s // unit) * unit
    nb = s_pad // unit
    pad = ((0, 0), (0, s_pad - s), (0, 0), (0, 0))

    def blocks(t):
        return jnp.pad(t, pad).reshape(b, nb, ATTN_BLOCK, dilation, h, dh)

    def with_prev(t):
        prev = jnp.pad(t[:, :-1], ((0, 0), (1, 0), (0, 0), (0, 0), (0, 0), (0, 0)))
        return jnp.concatenate([prev, t], axis=2)

    qb = blocks(q)
    kc = with_prev(blocks(k))
    vc = with_prev(blocks(v))
    scores = jnp.einsum('bnqrhd,bnkrhd->bnrhqk', qb, kc).astype(jnp.float32)
    qi = jnp.arange(ATTN_BLOCK)[:, None]
    kj = jnp.arange(2 * ATTN_BLOCK)[None, :]
    dist = qi + ATTN_BLOCK - kj
    blk = jnp.arange(nb)[:, None, None]
    valid = (dist >= 0) & (dist <= n_back) & (blk * ATTN_BLOCK + kj - ATTN_BLOCK >= 0)
    valid = valid[None, :, None, None]
    scores = jnp.where(valid, scores, -jnp.inf)
    m = jnp.max(scores, axis=-1, keepdims=True)
    p = jnp.exp(scores - m)
    l = jnp.sum(p, axis=-1)
    o = jnp.einsum('bnrhqk,bnkrhd->bnqrhd', p, vc.astype(jnp.float32))
    o = o / jnp.transpose(l, (0, 1, 4, 2, 3))[..., None]
    lse = jnp.transpose(m[..., 0] + jnp.log(l), (0, 1, 4, 2, 3))
    o = o.reshape(b, s_pad, h, dh)[:, :s]
    lse = lse.reshape(b, s_pad, h)[:, :s]
    return o, lse


def dilated_mixture(q, k, v):
    outs, lses = [], []
    for window, dilation in DILATED_PATTERNS:
        o, lse = dilated_branch(q, k, v, window, dilation)
        outs.append(o)
        lses.append(lse)
    w = jax.nn.softmax(jnp.stack(lses, axis=0), axis=0)
    out = jnp.sum(w[..., None] * jnp.stack(outs, axis=0), axis=0)
    b, s = q.shape[0], q.shape[1]
    return out.reshape(b, s, ATTN_WIDTH)


def conformer_conv(u, conv_w, conv_b, ln_g, ln_b):
    a, gate = jnp.split(u, 2, axis=-1)
    y = a * jax.nn.sigmoid(gate)
    rhs = conv_w[:, None, :].astype(y.dtype)
    y = lax.conv_general_dilated(y, rhs, window_strides=(1,), padding=[(CONV_KERNEL - 1, 0)],
                                 dimension_numbers=('NWC', 'WIO', 'NWC'),
                                 feature_group_count=CONV_WIDTH)
    yf = (y + conv_b).astype(jnp.float32)
    mu = jnp.mean(yf, axis=-1, keepdims=True)
    var = jnp.mean(jnp.square(yf - mu), axis=-1, keepdims=True)
    yf = (yf - mu) * lax.rsqrt(var + EPS) * ln_g.astype(jnp.float32) + ln_b.astype(jnp.float32)
    return jax.nn.silu(yf).astype(u.dtype)


def even_layer(x, positions, norm_g, w_in, q_norm, k_norm, conv_w, conv_b, ln_g, ln_b, w_out):
    b, s, _ = x.shape
    h = rms_norm(x, norm_g)
    proj = h @ w_in
    q, k, v, u = jnp.split(proj, [ATTN_WIDTH, 2 * ATTN_WIDTH, 3 * ATTN_WIDTH], axis=-1)
    shp = (b, s, N_HEADS_ATTN, HEAD_DIM)
    q = rope(rms_norm(q.reshape(shp), q_norm), positions) * (HEAD_DIM ** -0.5)
    k = rope(rms_norm(k.reshape(shp), k_norm), positions)
    attn = dilated_mixture(q, k, v.reshape(shp)).astype(x.dtype)
    conv = conformer_conv(u, conv_w, conv_b, ln_g, ln_b)
    return x + jnp.concatenate([attn, conv], axis=-1) @ w_out


def odd_layer(x, norm_g, pool_w, pool_b, pool_scale):
    b, s, d = x.shape
    hf = rms_norm(x, norm_g).astype(jnp.float32)
    cs0 = jnp.pad(jnp.cumsum(hf, axis=1), ((0, 0), (1, 0), (0, 0)))
    t1 = jnp.arange(s) + 1
    outs = []
    for g, w in enumerate(POOL_WINDOWS):
        sl = slice(g * POOL_GROUP, (g + 1) * POOL_GROUP)
        c = cs0[..., sl]
        lo = jnp.pad(c, ((0, 0), (w - 1, 0), (0, 0)))[:, :s]
        cnt = jnp.minimum(t1, w).astype(jnp.float32)[None, :, None]
        outs.append((c[:, 1:] - lo) / cnt - hf[..., sl])
    mixed = jnp.stack(outs, axis=2)
    y = jnp.einsum('bsgc,gce->bsge', mixed, pool_w.astype(jnp.float32)).reshape(b, s, d)
    y = (y + pool_b.astype(jnp.float32)) * pool_scale.astype(jnp.float32)
    return x + y.astype(x.dtype)


def hier_moe(x, norm_g, w_rg, b_rg, w_re, b_re, w_gate, w_up, w_down):
    b, s, d = x.shape
    t = rms_norm(x, norm_g).reshape(b * s, d)
    group_logits = (t @ w_rg).astype(jnp.float32) + b_rg.astype(jnp.float32)
    p_group = jax.nn.softmax(group_logits, axis=-1)
    g_star = jnp.argmax(group_logits, axis=-1)
    p_sel = jnp.take_along_axis(p_group, g_star[:, None], axis=1)[:, 0]
    exp_logits = ((t @ w_re).astype(jnp.float32) + b_re.astype(jnp.float32)).reshape(
        -1, N_GROUPS, EXPERTS_PER_GROUP)
    inner = jnp.take_along_axis(exp_logits, g_star[:, None, None], axis=1)[:, 0]
    top_vals, top_idx = lax.top_k(inner, TOP_K_INNER)
    w = jax.nn.softmax(top_vals, axis=-1) * p_sel[:, None]
    eid = g_star[:, None] * EXPERTS_PER_GROUP + top_idx
    gate = jnp.sum(jax.nn.one_hot(eid, N_EXPERTS, dtype=jnp.float32) * w[..., None], axis=1)
    gate = gate.astype(t.dtype)
    out = jnp.zeros_like(t)
    for e in range(N_EXPERTS):
        hid = jax.nn.silu(t @ w_gate[e]) * (t @ w_up[e])
        out = out + gate[:, e:e + 1] * (hid @ w_down[e])
    return x + out.reshape(b, s, d)


def setup_inputs(seed: int = 0) -> dict:
    key = jax.random.key(seed)
    ks = jax.random.split(key, 24)
    f32 = jnp.float32
    nrm = lambda k, shape, scale: jax.random.normal(k, shape, f32) * scale
    gain = lambda k, shape: 1.0 + 0.02 * jax.random.normal(k, shape, f32)
    x = jax.random.normal(ks[0], (BATCH, SEQ, D_MODEL), f32)
    offsets = jax.random.randint(ks[1], (BATCH, 1), 0, 4096, dtype=jnp.int32)
    positions = offsets + jnp.arange(SEQ, dtype=jnp.int32)[None, :]
    return {
        "x": x,
        "positions": positions,
        "even_norm": gain(ks[2], (N_EVEN, D_MODEL)),
        "w_in": nrm(ks[3], (N_EVEN, D_MODEL, IN_PROJ_WIDTH), D_MODEL ** -0.5),
        "q_norm": gain(ks[4], (N_EVEN, HEAD_DIM)),
        "k_norm": gain(ks[5], (N_EVEN, HEAD_DIM)),
        "conv_w": nrm(ks[6], (N_EVEN, CONV_KERNEL, CONV_WIDTH), CONV_KERNEL ** -0.5),
        "conv_b": nrm(ks[7], (N_EVEN, CONV_WIDTH), 0.02),
        "conv_ln_g": gain(ks[8], (N_EVEN, CONV_WIDTH)),
        "conv_ln_b": nrm(ks[9], (N_EVEN, CONV_WIDTH), 0.02),
        "w_out": nrm(ks[10], (N_EVEN, D_MODEL, D_MODEL), D_MODEL ** -0.5),
        "odd_norm": gain(ks[11], (N_ODD, D_MODEL)),
        "pool_w": nrm(ks[12], (N_ODD, len(POOL_WINDOWS), POOL_GROUP, POOL_GROUP), POOL_GROUP ** -0.5),
        "pool_b": nrm(ks[13], (N_ODD, D_MODEL), 0.02),
        "pool_scale": gain(ks[14], (N_ODD, D_MODEL)),
        "ffn_norm": gain(ks[15], (DEPTH, D_MODEL)),
        "w_router_group": nrm(ks[16], (DEPTH, D_MODEL, N_GROUPS), D_MODEL ** -0.5),
        "b_router_group": nrm(ks[17], (DEPTH, N_GROUPS), 0.01),
        "w_router_expert": nrm(ks[18], (DEPTH, D_MODEL, N_EXPERTS), D_MODEL ** -0.5),
        "b_router_expert": nrm(ks[19], (DEPTH, N_EXPERTS), 0.01),
        "w_expert_gate": nrm(ks[20], (DEPTH, N_EXPERTS, D_MODEL, EXPERT_HIDDEN), D_MODEL ** -0.5),
        "w_expert_up": nrm(ks[21], (DEPTH, N_EXPERTS, D_MODEL, EXPERT_HIDDEN), D_MODEL ** -0.5),
        "w_expert_down": nrm(ks[22], (DEPTH, N_EXPERTS, EXPERT_HIDDEN, D_MODEL), EXPERT_HIDDEN ** -0.5),
    }


def reference(x, positions, even_norm, w_in, q_norm, k_norm, conv_w, conv_b, conv_ln_g, conv_ln_b,
              w_out, odd_norm, pool_w, pool_b, pool_scale, ffn_norm, w_router_group, b_router_group,
              w_router_expert, b_router_expert, w_expert_gate, w_expert_up, w_expert_down):
    for i in range(DEPTH):
        j = i // 2
        if i % 2 == 0:
            x = even_layer(x, positions, even_norm[j], w_in[j], q_norm[j], k_norm[j], conv_w[j],
                           conv_b[j], conv_ln_g[j], conv_ln_b[j], w_out[j])
        else:
            x = odd_layer(x, odd_norm[j], pool_w[j], pool_b[j], pool_scale[j])
        x = hier_moe(x, ffn_norm[i], w_router_group[i], b_router_group[i], w_router_expert[i],
                     b_router_expert[i], w_expert_gate[i], w_expert_up[i], w_expert_down[i])
    return x
```

```python
import functools

import jax
import jax.numpy as jnp
from jax import lax
from jax.experimental import pallas as pl
from jax.experimental.pallas import tpu as pltpu

F32 = jnp.float32
BF16 = jnp.bfloat16
I32 = jnp.int32

D_MODEL = 2048
HEAD_DIM = 128
N_HEADS = 8
ATTN_WIDTH = N_HEADS * HEAD_DIM
ATTN_BLOCK = 128
DILATIONS = (1, 4, 16)
ATTN_CHUNK = ATTN_BLOCK * max(DILATIONS)
CONV_WIDTH = D_MODEL - ATTN_WIDTH
CONV_KERNEL = 31
CONV_HALO = 32
IN_PROJ_WIDTH = 3 * ATTN_WIDTH + 2 * CONV_WIDTH
POOL_WINDOWS = (2, 4, 8, 16)
POOL_GROUP = D_MODEL // len(POOL_WINDOWS)
POOL_HALO = 16
N_GROUPS = 4
EXPERTS_PER_GROUP = 8
N_EXPERTS = N_GROUPS * EXPERTS_PER_GROUP
EXPERT_HIDDEN = 512
ROPE_THETA = 10000.0
EPS = 1e-6
LANES = 128
NEG = -1e30

TM_PROJ = 512
TN_PROJ = 1024
TM_OUT = 256
TC_CONV = 512
RC_CONV = 32
CONV_COLS = 256
SUBLANES = 8
TM_EXP = 256
TD_DISP = 512
TC_COMB = 256
VMEM_LIMIT = 56 * 1024 * 1024


def _cparams(n_axes):
    return pltpu.CompilerParams(dimension_semantics=("arbitrary",) * n_axes,
                                vmem_limit_bytes=VMEM_LIMIT)


def _rms(x, g):
    return x * lax.rsqrt(jnp.mean(x * x, axis=-1, keepdims=True) + EPS) * g


def _inproj_body(x_ref, pos_ref, g_ref, w_ref, qn_ref, kn_ref, invf_ref,
                 q_ref, k_ref, v_ref, u_ref, hn_sc, cos_sc, sin_sc):
    j = pl.program_id(1)

    @pl.when(j == 0)
    def _():
        hn_sc[...] = _rms(x_ref[...], g_ref[...]).astype(BF16)
        ang = pos_ref[...].astype(F32) * invf_ref[...]
        lane = lax.broadcasted_iota(I32, ang.shape, 1)
        cos_sc[...] = jnp.cos(ang)
        sin_sc[...] = jnp.where(lane < HEAD_DIM // 2, -1.0, 1.0) * jnp.sin(ang)

    acc = jnp.dot(hn_sc[...], w_ref[...], preferred_element_type=F32)

    def qk_heads(norm_ref, out_ref, scale):
        cs = cos_sc[...]
        sn = sin_sc[...]
        for h in range(N_HEADS):
            y = _rms(acc[:, h * HEAD_DIM:(h + 1) * HEAD_DIM], norm_ref[...])
            y = y * cs + pltpu.roll(y, HEAD_DIM // 2, 1) * sn
            out_ref[0, h] = (y * scale).astype(BF16)

    @pl.when(j == 0)
    def _():
        qk_heads(qn_ref, q_ref, HEAD_DIM ** -0.5)

    @pl.when(j == 1)
    def _():
        qk_heads(kn_ref, k_ref, 1.0)

    @pl.when(j == 2)
    def _():
        for h in range(N_HEADS):
            v_ref[0, h] = acc[:, h * HEAD_DIM:(h + 1) * HEAD_DIM].astype(BF16)

    @pl.when(j == 3)
    def _():
        u_ref[:, :TN_PROJ] = acc

    @pl.when(j == 4)
    def _():
        u_ref[:, TN_PROJ:] = acc


def _inproj(x2, pos2, g, w_bf, qn, kn, invf2, batch, seq):
    t_tokens = batch * seq
    nsb = seq // TM_PROJ
    head_spec = pl.BlockSpec((1, N_HEADS, TM_PROJ, HEAD_DIM), lambda i, j: (i // nsb, 0, i % nsb, 0))
    head_shape = jax.ShapeDtypeStruct((batch, N_HEADS, seq, HEAD_DIM), BF16)
    return pl.pallas_call(
        _inproj_body,
        grid=(t_tokens // TM_PROJ, IN_PROJ_WIDTH // TN_PROJ),
        in_specs=[
            pl.BlockSpec((TM_PROJ, D_MODEL), lambda i, j: (i, 0)),
            pl.BlockSpec((TM_PROJ, 1), lambda i, j: (i, 0)),
            pl.BlockSpec((1, D_MODEL), lambda i, j: (0, 0)),
            pl.BlockSpec((D_MODEL, TN_PROJ), lambda i, j: (0, j)),
            pl.BlockSpec((1, HEAD_DIM), lambda i, j: (0, 0)),
            pl.BlockSpec((1, HEAD_DIM), lambda i, j: (0, 0)),
            pl.BlockSpec((1, HEAD_DIM), lambda i, j: (0, 0)),
        ],
        out_specs=[head_spec, head_spec, head_spec,
                   pl.BlockSpec((TM_PROJ, 2 * CONV_WIDTH), lambda i, j: (i, 0))],
        out_shape=[head_shape, head_shape, head_shape,
                   jax.ShapeDtypeStruct((t_tokens, 2 * CONV_WIDTH), F32)],
        scratch_shapes=[pltpu.VMEM((TM_PROJ, D_MODEL), BF16),
                        pltpu.VMEM((TM_PROJ, HEAD_DIM), F32),
                        pltpu.VMEM((TM_PROJ, HEAD_DIM), F32)],
        compiler_params=_cparams(2),
        name="inproj",
    )(x2, pos2, g, w_bf, qn, kn, invf2)


def _attn_body(q_ref, ko_ref, kp_ref, vo_ref, vp_ref, o_ref, qf, kf, vf, m_sc, l_sc, acc_sc):
    c = pl.program_id(2)
    ch = ATTN_CHUNK
    qf[...] = q_ref[0, 0].astype(F32)
    kf[:ch] = kp_ref[0, 0].astype(F32)
    kf[ch:] = ko_ref[0, 0].astype(F32)
    vf[:ch] = vp_ref[0, 0].astype(F32)
    vf[ch:] = vo_ref[0, 0].astype(F32)

    qi = lax.broadcasted_iota(I32, (ATTN_BLOCK, 2 * ATTN_BLOCK), 0)
    kj = lax.broadcasted_iota(I32, (ATTN_BLOCK, 2 * ATTN_BLOCK), 1)
    band = (kj >= qi) & (kj <= qi + ATTN_BLOCK)
    bias_all = jnp.where(band, 0.0, NEG)
    bias_own = jnp.where(band & (kj >= ATTN_BLOCK), 0.0, NEG)

    def rows(start, n, d):
        return pl.ds(start, n) if d == 1 else pl.ds(start, n, stride=d)

    def block(t, d, first):
        unit = ATTN_BLOCK * d
        qs = (t // d) * unit + (t % d)
        ks = ch + qs - unit
        q = qf[rows(qs, ATTN_BLOCK, d), :].astype(BF16)
        kc = kf[rows(ks, 2 * ATTN_BLOCK, d), :].astype(BF16)
        vc = vf[rows(ks, 2 * ATTN_BLOCK, d), :].astype(BF16)
        s = lax.dot_general(q, kc, (((1,), (1,)), ((), ())), preferred_element_type=F32)
        has_prev = jnp.logical_or(c > 0, ks >= ch)
        s = s + jnp.where(has_prev, bias_all, bias_own)
        m_b = jnp.max(s, axis=-1, keepdims=True)
        p = jnp.exp(s - m_b)
        l_b = jnp.sum(p, axis=-1, keepdims=True)
        pv = jnp.dot(p.astype(BF16), vc, preferred_element_type=F32)
        r = rows(qs, ATTN_BLOCK, d)
        if first:
            m_sc[r, :] = m_b
            l_sc[r, :] = l_b
            acc_sc[r, :] = pv
        else:
            m_o = m_sc[r, :]
            m_n = jnp.maximum(m_o, m_b)
            a_o = jnp.exp(m_o - m_n)
            a_b = jnp.exp(m_b - m_n)
            m_sc[r, :] = m_n
            l_sc[r, :] = a_o * l_sc[r, :] + a_b * l_b
            acc_sc[r, :] = a_o * acc_sc[r, :] + a_b * pv

    for idx, d in enumerate(DILATIONS):
        def step(t, carry, d=d, first=(idx == 0)):
            block(t, d, first)
            return carry
        lax.fori_loop(0, ch // ATTN_BLOCK, step, 0)

    o_ref[0, 0] = (acc_sc[...] / l_sc[...]).astype(BF16)


def _attention(q, k, v, batch, seq):
    ch = ATTN_CHUNK
    own = pl.BlockSpec((1, 1, ch, HEAD_DIM), lambda b, h, c: (b, h, c, 0))
    prev = pl.BlockSpec((1, 1, ch, HEAD_DIM), lambda b, h, c: (b, h, jnp.maximum(c - 1, 0), 0))
    return pl.pallas_call(
        _attn_body,
        grid=(batch, N_HEADS, seq // ch),
        in_specs=[own, own, prev, own, prev],
        out_specs=own,
        out_shape=jax.ShapeDtypeStruct((batch, N_HEADS, seq, HEAD_DIM), BF16),
        scratch_shapes=[pltpu.VMEM((ch, HEAD_DIM), F32),
                        pltpu.VMEM((2 * ch, HEAD_DIM), F32),
                        pltpu.VMEM((2 * ch, HEAD_DIM), F32),
                        pltpu.VMEM((ch, 1), F32),
                        pltpu.VMEM((ch, 1), F32),
                        pltpu.VMEM((ch, HEAD_DIM), F32)],
        compiler_params=_cparams(3),
        name="dilated_attn",
    )(q, k, k, v, v)


def _conv_body(um_ref, uh_ref, w_ref, b_ref, g_ref, bb_ref, o_ref, y_sc, z_sc):
    i = pl.program_id(1)
    cw = CONV_WIDTH
    uh = uh_ref[...]
    yh = uh[:, :cw] * jax.nn.sigmoid(uh[:, cw:])
    y_sc[:CONV_HALO] = jnp.where(i > 0, yh, 0.0)

    def glu(rc, carry):
        r0 = pl.multiple_of(rc * RC_CONV, RC_CONV)
        um = um_ref[pl.ds(r0, RC_CONV), :]
        y_sc[pl.ds(CONV_HALO + r0, RC_CONV), :] = um[:, :cw] * jax.nn.sigmoid(um[:, cw:])
        return carry

    lax.fori_loop(0, TC_CONV // RC_CONV, glu, 0)

    win_rows = RC_CONV + CONV_HALO
    first_off = CONV_HALO - (CONV_KERNEL - 1)

    def chunk(rc, carry):
        r0 = pl.multiple_of(rc * RC_CONV, RC_CONV)
        for cg in range(cw // CONV_COLS):
            cols = slice(cg * CONV_COLS, (cg + 1) * CONV_COLS)
            win = y_sc[pl.ds(r0, win_rows), cols]
            acc = jnp.zeros((RC_CONV, CONV_COLS), F32)
            for phase in range(SUBLANES):
                shifted = win if phase == 0 else pltpu.roll(win, win_rows - phase, 0)
                for k in range(CONV_KERNEL):
                    off = first_off + k
                    if off % SUBLANES == phase:
                        base = off - phase
                        acc = acc + shifted[base:base + RC_CONV] * w_ref[k:k + 1, cols]
            z_sc[:, cols] = acc
        yf = z_sc[...] + b_ref[...]
        mu = jnp.mean(yf, axis=-1, keepdims=True)
        yc = yf - mu
        var = jnp.mean(yc * yc, axis=-1, keepdims=True)
        yn = yc * lax.rsqrt(var + EPS) * g_ref[...] + bb_ref[...]
        o_ref[pl.ds(r0, RC_CONV), :] = (yn * jax.nn.sigmoid(yn)).astype(BF16)
        return carry

    lax.fori_loop(0, TC_CONV // RC_CONV, chunk, 0)


def _conv(u, conv_w, conv_b, ln_g, ln_b, batch, seq):
    nsb = seq // TC_CONV
    hpt = TC_CONV // CONV_HALO
    row = lambda: pl.BlockSpec((1, CONV_WIDTH), lambda b, i: (0, 0))
    return pl.pallas_call(
        _conv_body,
        grid=(batch, nsb),
        in_specs=[
            pl.BlockSpec((TC_CONV, 2 * CONV_WIDTH), lambda b, i: (b * nsb + i, 0)),
            pl.BlockSpec((CONV_HALO, 2 * CONV_WIDTH),
                         lambda b, i: (jnp.maximum((b * nsb + i) * hpt - 1, 0), 0)),
            pl.BlockSpec((CONV_KERNEL, CONV_WIDTH), lambda b, i: (0, 0)),
            row(), row(), row(),
        ],
        out_specs=pl.BlockSpec((TC_CONV, CONV_WIDTH), lambda b, i: (b * nsb + i, 0)),
        out_shape=jax.ShapeDtypeStruct((batch * seq, CONV_WIDTH), BF16),
        scratch_shapes=[pltpu.VMEM((CONV_HALO + TC_CONV, CONV_WIDTH), F32),
                        pltpu.VMEM((RC_CONV, CONV_WIDTH), F32)],
        compiler_params=_cparams(2),
        name="conformer_conv",
    )(u, u, conv_w, conv_b, ln_g, ln_b)


def _router(x1, fg_ref, wr_ref, br_ref, info_ref, wts_ref, cnt_ref, carry_sc):
    tm = x1.shape[0]

    @pl.when(pl.program_id(0) == 0)
    def _():
        carry_sc[...] = jnp.zeros_like(carry_sc)

    t = _rms(x1, fg_ref[...]).astype(BF16)
    logits = jnp.dot(t, wr_ref[...], preferred_element_type=F32) + br_ref[...]
    lane = lax.broadcasted_iota(I32, (tm, LANES), 1)
    lanef = lane.astype(F32)
    ninf = -jnp.inf

    def first_argmax(vals):
        top = jnp.max(vals, axis=-1, keepdims=True)
        idx = jnp.min(jnp.where(vals == top, lanef, float(LANES)), axis=-1, keepdims=True)
        return top, idx

    gl = jnp.where(lane < N_GROUPS, logits, ninf)
    gmax, gidx = first_argmax(gl)
    p_sel = 1.0 / jnp.sum(jnp.exp(gl - gmax), axis=-1, keepdims=True)
    lo = N_GROUPS + EXPERTS_PER_GROUP * gidx
    el = jnp.where((lanef >= lo) & (lanef < lo + EXPERTS_PER_GROUP), logits, ninf)
    v1, i1 = first_argmax(el)
    el2 = jnp.where(lanef == i1, ninf, el)
    v2, i2 = first_argmax(el2)
    e21 = jnp.exp(v2 - v1)
    w1 = p_sel / (1.0 + e21)
    w2 = p_sel * e21 / (1.0 + e21)

    hit1 = lanef == i1
    hit2 = lanef == i2
    mh = jnp.where(hit1 | hit2, 1.0, 0.0)
    ri = lax.broadcasted_iota(I32, (tm, tm), 0)
    ci = lax.broadcasted_iota(I32, (tm, tm), 1)
    tri = jnp.where(ci < ri, 1.0, 0.0).astype(BF16)
    before = jnp.dot(tri, mh.astype(BF16), preferred_element_type=F32) + carry_sc[...]
    rank1 = jnp.sum(jnp.where(hit1, before, 0.0), axis=-1, keepdims=True)
    rank2 = jnp.sum(jnp.where(hit2, before, 0.0), axis=-1, keepdims=True)
    carry_sc[...] = carry_sc[...] + jnp.sum(mh, axis=0, keepdims=True)
    cnt_ref[...] = carry_sc[...]

    info = jnp.where(lane == 0, i1 - N_GROUPS,
                     jnp.where(lane == 1, i2 - N_GROUPS,
                               jnp.where(lane == 2, rank1, jnp.where(lane == 3, rank2, 0.0))))
    info_ref[...] = info.astype(I32)
    wts_ref[...] = jnp.where(lane == 0, w1, jnp.where(lane == 1, w2, 0.0))


def _router_specs(tm):
    in_specs = [pl.BlockSpec((1, D_MODEL), lambda i: (0, 0)),
                pl.BlockSpec((D_MODEL, LANES), lambda i: (0, 0)),
                pl.BlockSpec((1, LANES), lambda i: (0, 0))]
    out_specs = [pl.BlockSpec((tm, LANES), lambda i: (i, 0)),
                 pl.BlockSpec((tm, LANES), lambda i: (i, 0)),
                 pl.BlockSpec((1, LANES), lambda i: (0, 0))]
    return in_specs, out_specs


def _router_shapes(t_tokens):
    return [jax.ShapeDtypeStruct((t_tokens, LANES), I32),
            jax.ShapeDtypeStruct((t_tokens, LANES), F32),
            jax.ShapeDtypeStruct((1, LANES), F32)]


def _outproj_body(attn_ref, conv_ref, x_ref, wo_ref, fg_ref, wr_ref, br_ref,
                  x1_ref, info_ref, wts_ref, cnt_ref, carry_sc):
    cat = jnp.concatenate([attn_ref[0, h] for h in range(N_HEADS)] + [conv_ref[...]], axis=-1)
    x1 = x_ref[...] + jnp.dot(cat, wo_ref[...], preferred_element_type=F32)
    x1_ref[...] = x1
    _router(x1, fg_ref, wr_ref, br_ref, info_ref, wts_ref, cnt_ref, carry_sc)


def _outproj_router(attn, conv, x2, wo_bf, fg, wr_bf, br, batch, seq):
    t_tokens = batch * seq
    tm = TM_OUT
    nsb = seq // tm
    r_in, r_out = _router_specs(tm)
    return pl.pallas_call(
        _outproj_body,
        grid=(t_tokens // tm,),
        in_specs=[pl.BlockSpec((1, N_HEADS, tm, HEAD_DIM), lambda i: (i // nsb, 0, i % nsb, 0)),
                  pl.BlockSpec((tm, CONV_WIDTH), lambda i: (i, 0)),
                  pl.BlockSpec((tm, D_MODEL), lambda i: (i, 0)),
                  pl.BlockSpec((D_MODEL, D_MODEL), lambda i: (0, 0))] + r_in,
        out_specs=[pl.BlockSpec((tm, D_MODEL), lambda i: (i, 0))] + r_out,
        out_shape=[jax.ShapeDtypeStruct((t_tokens, D_MODEL), F32)] + _router_shapes(t_tokens),
        scratch_shapes=[pltpu.VMEM((1, LANES), F32)],
        compiler_params=_cparams(1),
        name="outproj_router",
    )(attn, conv, x2, wo_bf, fg, wr_bf, br)


def _pool_body(x_ref, xh_ref, og_ref, pw_ref, pb_ref, ps_ref, fg_ref, wr_ref, br_ref,
               x1_ref, info_ref, wts_ref, cnt_ref, carry_sc, h_sc, *, tiles_per_seq):
    tm = x_ref.shape[0]
    ts = pl.program_id(0) % tiles_per_seq
    x = x_ref[...]
    h_sc[:POOL_HALO] = jnp.where(ts > 0, _rms(xh_ref[...], og_ref[...]), 0.0)
    h_sc[POOL_HALO:] = _rms(x, og_ref[...])
    tpos = ts * tm + lax.broadcasted_iota(I32, (tm, 1), 0) + 1
    ys = []
    for g, w in enumerate(POOL_WINDOWS):
        cols = slice(g * POOL_GROUP, (g + 1) * POOL_GROUP)
        cur = h_sc[POOL_HALO:, cols]
        tot = cur
        for back in range(1, w):
            tot = tot + h_sc[POOL_HALO - back:POOL_HALO - back + tm, cols]
        cnt = jnp.minimum(tpos, w).astype(F32)
        mixed = tot / cnt - cur
        ys.append(jnp.dot(mixed.astype(BF16), pw_ref[g], preferred_element_type=F32))
    y = jnp.concatenate(ys, axis=-1)
    x1 = x + (y + pb_ref[...]) * ps_ref[...]
    x1_ref[...] = x1
    _router(x1, fg_ref, wr_ref, br_ref, info_ref, wts_ref, cnt_ref, carry_sc)


def _pool_router(x2, og, pw_bf, pb, ps, fg, wr_bf, br, batch, seq):
    t_tokens = batch * seq
    tm = TM_OUT
    hpt = tm // POOL_HALO
    r_in, r_out = _router_specs(tm)
    vec = lambda: pl.BlockSpec((1, D_MODEL), lambda i: (0, 0))
    return pl.pallas_call(
        functools.partial(_pool_body, tiles_per_seq=seq // tm),
        grid=(t_tokens // tm,),
        in_specs=[pl.BlockSpec((tm, D_MODEL), lambda i: (i, 0)),
                  pl.BlockSpec((POOL_HALO, D_MODEL), lambda i: (jnp.maximum(i * hpt - 1, 0), 0)),
                  vec(),
                  pl.BlockSpec((len(POOL_WINDOWS), POOL_GROUP, POOL_GROUP), lambda i: (0, 0, 0)),
                  vec(), vec()] + r_in,
        out_specs=[pl.BlockSpec((tm, D_MODEL), lambda i: (i, 0))] + r_out,
        out_shape=[jax.ShapeDtypeStruct((t_tokens, D_MODEL), F32)] + _router_shapes(t_tokens),
        scratch_shapes=[pltpu.VMEM((1, LANES), F32),
                        pltpu.VMEM((POOL_HALO + tm, D_MODEL), F32)],
        compiler_params=_cparams(1),
        name="pool_router",
    )(x2, x2, og, pw_bf, pb, ps, fg, wr_bf, br)


def _row_copy(src_hbm, src_row, dst, dst_row, sem):
    return pltpu.make_async_copy(src_hbm.at[pl.ds(src_row, 1)], dst.at[pl.ds(dst_row, 1)], sem)


def _dispatch_body(pos_ref, x_hbm, xs_hbm, sem):
    base = pl.program_id(0) * TD_DISP

    def issue(r, carry):
        for k in range(2):
            _row_copy(x_hbm, base + r, xs_hbm, pos_ref[0, 0, 2 * r + k], sem).start()
        return carry

    lax.fori_loop(0, TD_DISP, issue, 0)

    def drain(r, carry):
        for k in range(2):
            _row_copy(x_hbm, 0, xs_hbm, 0, sem).wait()
        return carry

    lax.fori_loop(0, TD_DISP, drain, 0)


def _dispatch(pos3, x1, n_rows):
    t_tokens = x1.shape[0]
    return pl.pallas_call(
        _dispatch_body,
        grid=(t_tokens // TD_DISP,),
        in_specs=[pl.BlockSpec((1, 1, 2 * TD_DISP), lambda i: (i, 0, 0), memory_space=pltpu.SMEM),
                  pl.BlockSpec(memory_space=pl.ANY)],
        out_specs=pl.BlockSpec(memory_space=pl.ANY),
        out_shape=jax.ShapeDtypeStruct((n_rows, D_MODEL), F32),
        scratch_shapes=[pltpu.SemaphoreType.DMA(())],
        compiler_params=_cparams(1),
        name="moe_dispatch",
    )(pos3, x1)


def _expert_body(te_ref, nu_ref, xs_ref, fg_ref, wg_ref, wu_ref, wd_ref, ys_ref, wg_sc, wu_sc, wd_sc):
    j = pl.program_id(0)

    @pl.when(j < nu_ref[0])
    def _():
        prev = te_ref[jnp.maximum(j - 1, 0)]

        @pl.when(jnp.logical_or(j == 0, te_ref[j] != prev))
        def _():
            wg_sc[...] = wg_ref[0].astype(BF16)
            wu_sc[...] = wu_ref[0].astype(BF16)
            wd_sc[...] = wd_ref[0].astype(BF16)

        t = _rms(xs_ref[...], fg_ref[...]).astype(BF16)
        a = jnp.dot(t, wg_sc[...], preferred_element_type=F32)
        b = jnp.dot(t, wu_sc[...], preferred_element_type=F32)
        hid = (a * jax.nn.sigmoid(a) * b).astype(BF16)
        ys_ref[...] = jnp.dot(hid, wd_sc[...], preferred_element_type=F32)


def _experts(tile_expert, n_used, xs, fg, wg, wu, wd):
    n_rows = xs.shape[0]
    n_tiles = n_rows // TM_EXP
    row_idx = lambda j, te, nu: (jnp.minimum(j, nu[0] - 1), 0)
    return pl.pallas_call(
        _expert_body,
        grid_spec=pltpu.PrefetchScalarGridSpec(
            num_scalar_prefetch=2,
            grid=(n_tiles,),
            in_specs=[pl.BlockSpec((TM_EXP, D_MODEL), row_idx),
                      pl.BlockSpec((1, D_MODEL), lambda j, te, nu: (0, 0)),
                      pl.BlockSpec((1, D_MODEL, EXPERT_HIDDEN), lambda j, te, nu: (te[j], 0, 0)),
                      pl.BlockSpec((1, D_MODEL, EXPERT_HIDDEN), lambda j, te, nu: (te[j], 0, 0)),
                      pl.BlockSpec((1, EXPERT_HIDDEN, D_MODEL), lambda j, te, nu: (te[j], 0, 0))],
            out_specs=pl.BlockSpec((TM_EXP, D_MODEL), row_idx),
            scratch_shapes=[pltpu.VMEM((D_MODEL, EXPERT_HIDDEN), BF16),
                            pltpu.VMEM((D_MODEL, EXPERT_HIDDEN), BF16),
                            pltpu.VMEM((EXPERT_HIDDEN, D_MODEL), BF16)]),
        out_shape=jax.ShapeDtypeStruct((n_rows, D_MODEL), F32),
        compiler_params=_cparams(1),
        name="moe_experts",
    )(tile_expert, n_used, xs, fg, wg, wu, wd)


def _combine_body(pos_ref, posn_ref, x_ref, wts_ref, ys_hbm, o_ref, gbuf, sems, *, n_steps):
    i = pl.program_id(0)
    slot = i % 2

    def issue(p_ref, s):
        def body(r, carry):
            for k in range(2):
                _row_copy(ys_hbm, p_ref[0, 0, 2 * r + k], gbuf.at[s, k], r, sems.at[s]).start()
            return carry
        lax.fori_loop(0, TC_COMB, body, 0)

    @pl.when(i == 0)
    def _():
        issue(pos_ref, 0)

    @pl.when(i + 1 < n_steps)
    def _():
        issue(posn_ref, 1 - slot)

    def drain(r, carry):
        for k in range(2):
            _row_copy(ys_hbm, 0, gbuf.at[slot, k], 0, sems.at[slot]).wait()
        return carry

    lax.fori_loop(0, TC_COMB, drain, 0)
    w = wts_ref[...]
    o_ref[...] = x_ref[...] + w[:, 0:1] * gbuf[slot, 0] + w[:, 1:2] * gbuf[slot, 1]


def _combine(pos3, x1, wts, ys):
    t_tokens = x1.shape[0]
    n_steps = t_tokens // TC_COMB
    return pl.pallas_call(
        functools.partial(_combine_body, n_steps=n_steps),
        grid=(n_steps,),
        in_specs=[pl.BlockSpec((1, 1, 2 * TC_COMB), lambda i: (i, 0, 0), memory_space=pltpu.SMEM),
                  pl.BlockSpec((1, 1, 2 * TC_COMB), lambda i: (jnp.minimum(i + 1, n_steps - 1), 0, 0),
                               memory_space=pltpu.SMEM),
                  pl.BlockSpec((TC_COMB, D_MODEL), lambda i: (i, 0)),
                  pl.BlockSpec((TC_COMB, LANES), lambda i: (i, 0)),
                  pl.BlockSpec(memory_space=pl.ANY)],
        out_specs=pl.BlockSpec((TC_COMB, D_MODEL), lambda i: (i, 0)),
        out_shape=jax.ShapeDtypeStruct((t_tokens, D_MODEL), F32),
        scratch_shapes=[pltpu.VMEM((2, 2, TC_COMB, D_MODEL), F32),
                        pltpu.SemaphoreType.DMA((2,))],
        compiler_params=_cparams(1),
        name="moe_combine",
    )(pos3, pos3, x1, wts, ys)


def _moe(x1, info, wts, counts, fg, wg, wu, wd):
    t_tokens = x1.shape[0]
    n_tiles = 2 * t_tokens // TM_EXP + N_EXPERTS
    cnt = counts[0, N_GROUPS:N_GROUPS + N_EXPERTS].astype(I32)
    padded = (cnt + TM_EXP - 1) // TM_EXP * TM_EXP
    seg_end = jnp.cumsum(padded)
    seg_start = seg_end - padded
    n_used = seg_end[-1] // TM_EXP
    tile_row = jnp.minimum(jnp.arange(n_tiles, dtype=I32), n_used - 1) * TM_EXP
    tile_expert = jnp.searchsorted(seg_end, tile_row, side="right").astype(I32)
    pos = seg_start[info[:, 0:2]] + info[:, 2:4]
    xs = _dispatch(pos.reshape(t_tokens // TD_DISP, 1, 2 * TD_DISP), x1, n_tiles * TM_EXP)
    ys = _experts(tile_expert, n_used.reshape(1), xs, fg, wg, wu, wd)
    return _combine(pos.reshape(t_tokens // TC_COMB, 1, 2 * TC_COMB), x1, wts, ys)


def _router_params(w_rg, b_rg, w_re, b_re):
    pad = LANES - N_GROUPS - N_EXPERTS
    wr = jnp.concatenate([w_rg, w_re, jnp.zeros((D_MODEL, pad), F32)], axis=1).astype(BF16)
    br = jnp.concatenate([b_rg, b_re, jnp.zeros((pad,), F32)]).reshape(1, LANES)
    return wr, br


def kernel(x, positions, even_norm, w_in, q_norm, k_norm, conv_w, conv_b, conv_ln_g, conv_ln_b, w_out,
           odd_norm, pool_w, pool_b, pool_scale, ffn_norm, w_router_group, b_router_group,
           w_router_expert, b_router_expert, w_expert_gate, w_expert_up, w_expert_down):
    batch, seq, d = x.shape
    depth = ffn_norm.shape[0]
    assert d == D_MODEL and seq % ATTN_CHUNK == 0 and seq % TM_PROJ == 0
    t_tokens = batch * seq
    half = HEAD_DIM // 2
    inv_freq = jnp.float32(ROPE_THETA) ** (-jnp.arange(half, dtype=F32) / half)
    invf2 = jnp.concatenate([inv_freq, inv_freq]).reshape(1, HEAD_DIM)
    pos2 = positions.reshape(t_tokens, 1)
    row = lambda a: a.reshape(1, -1)

    x2 = x.reshape(t_tokens, d)
    for i in range(depth):
        j = i // 2
        wr, br = _router_params(w_router_group[i], b_router_group[i], w_router_expert[i], b_router_expert[i])
        fg = row(ffn_norm[i])
        if i % 2 == 0:
            q, k, v, u = _inproj(x2, pos2, row(even_norm[j]), w_in[j].astype(BF16), row(q_norm[j]),
                                 row(k_norm[j]), invf2, batch, seq)
            attn = _attention(q, k, v, batch, seq)
            conv = _conv(u, conv_w[j], row(conv_b[j]), row(conv_ln_g[j]), row(conv_ln_b[j]), batch, seq)
            x1, info, wts, counts = _outproj_router(attn, conv, x2, w_out[j].astype(BF16), fg, wr, br,
                                                    batch, seq)
        else:
            x1, info, wts, counts = _pool_router(x2, row(odd_norm[j]), pool_w[j].astype(BF16),
                                                 row(pool_b[j]), row(pool_scale[j]), fg, wr, br, batch, seq)
        x2 = _moe(x1, info, wts, counts, fg, w_expert_gate[i], w_expert_up[i], w_expert_down[i])
    return x2.reshape(batch, seq, d)
```

```python
import functools

import jax
import jax.numpy as jnp
from jax import lax
from jax.experimental import pallas as pl
from jax.experimental.pallas import tpu as pltpu

F32 = jnp.float32
BF16 = jnp.bfloat16
I32 = jnp.int32

D_MODEL = 2048
HEAD_DIM = 128
N_HEADS = 8
ATTN_WIDTH = N_HEADS * HEAD_DIM
ATTN_BLOCK = 128
DILATIONS = (1, 4, 16)
ATTN_CHUNK = ATTN_BLOCK * max(DILATIONS)
ATTN_UNROLL = 4
MERGE_ROWS = 256
CONV_WIDTH = D_MODEL - ATTN_WIDTH
CONV_KERNEL = 31
CONV_HALO = 32
IN_PROJ_WIDTH = 3 * ATTN_WIDTH + 2 * CONV_WIDTH
POOL_WINDOWS = (2, 4, 8, 16)
POOL_GROUP = D_MODEL // len(POOL_WINDOWS)
POOL_HALO = 16
N_GROUPS = 4
EXPERTS_PER_GROUP = 8
N_EXPERTS = N_GROUPS * EXPERTS_PER_GROUP
EXPERT_HIDDEN = 512
ROPE_THETA = 10000.0
EPS = 1e-6
LANES = 128
NEG = -1e30

TM_PROJ = 512
TN_PROJ = 1024
TM_OUT = 256
TC_CONV = 512
RC_CONV = 32
CONV_COLS = 256
SUBLANES = 8
TM_EXP = 256
TD_DISP = 512
TC_COMB = 256
DMA_UNROLL = 8
VMEM_LIMIT = 56 * 1024 * 1024


def _cparams(n_axes):
    return pltpu.CompilerParams(dimension_semantics=("arbitrary",) * n_axes,
                                vmem_limit_bytes=VMEM_LIMIT)


def _rms(x, g):
    return x * lax.rsqrt(jnp.mean(x * x, axis=-1, keepdims=True) + EPS) * g


def _inproj_body(x_ref, pos_ref, g_ref, w_ref, qn_ref, kn_ref, invf_ref,
                 q_ref, k_ref, v_ref, u_ref, hn_sc, cos_sc, sin_sc):
    j = pl.program_id(1)

    @pl.when(j == 0)
    def _():
        hn_sc[...] = _rms(x_ref[...], g_ref[...]).astype(BF16)
        ang = pos_ref[...].astype(F32) * invf_ref[...]
        lane = lax.broadcasted_iota(I32, ang.shape, 1)
        cos_sc[...] = jnp.cos(ang)
        sin_sc[...] = jnp.where(lane < HEAD_DIM // 2, -1.0, 1.0) * jnp.sin(ang)

    acc = jnp.dot(hn_sc[...], w_ref[...], preferred_element_type=F32)

    def qk_heads(norm_ref, out_ref, scale):
        cs = cos_sc[...]
        sn = sin_sc[...]
        for h in range(N_HEADS):
            y = _rms(acc[:, h * HEAD_DIM:(h + 1) * HEAD_DIM], norm_ref[...])
            y = y * cs + pltpu.roll(y, HEAD_DIM // 2, 1) * sn
            out_ref[0, h] = (y * scale).astype(BF16)

    @pl.when(j == 0)
    def _():
        qk_heads(qn_ref, q_ref, HEAD_DIM ** -0.5)

    @pl.when(j == 1)
    def _():
        qk_heads(kn_ref, k_ref, 1.0)

    @pl.when(j == 2)
    def _():
        for h in range(N_HEADS):
            v_ref[0, h] = acc[:, h * HEAD_DIM:(h + 1) * HEAD_DIM].astype(BF16)

    @pl.when(j == 3)
    def _():
        u_ref[:, :TN_PROJ] = acc

    @pl.when(j == 4)
    def _():
        u_ref[:, TN_PROJ:] = acc


def _inproj(x2, pos2, g, w_bf, qn, kn, invf2, batch, seq):
    t_tokens = batch * seq
    nsb = seq // TM_PROJ
    head_spec = pl.BlockSpec((1, N_HEADS, TM_PROJ, HEAD_DIM), lambda i, j: (i // nsb, 0, i % nsb, 0))
    head_shape = jax.ShapeDtypeStruct((batch, N_HEADS, seq, HEAD_DIM), BF16)
    return pl.pallas_call(
        _inproj_body,
        grid=(t_tokens // TM_PROJ, IN_PROJ_WIDTH // TN_PROJ),
        in_specs=[
            pl.BlockSpec((TM_PROJ, D_MODEL), lambda i, j: (i, 0)),
            pl.BlockSpec((TM_PROJ, 1), lambda i, j: (i, 0)),
            pl.BlockSpec((1, D_MODEL), lambda i, j: (0, 0)),
            pl.BlockSpec((D_MODEL, TN_PROJ), lambda i, j: (0, j)),
            pl.BlockSpec((1, HEAD_DIM), lambda i, j: (0, 0)),
            pl.BlockSpec((1, HEAD_DIM), lambda i, j: (0, 0)),
            pl.BlockSpec((1, HEAD_DIM), lambda i, j: (0, 0)),
        ],
        out_specs=[head_spec, head_spec, head_spec,
                   pl.BlockSpec((TM_PROJ, 2 * CONV_WIDTH), lambda i, j: (i, 0))],
        out_shape=[head_shape, head_shape, head_shape,
                   jax.ShapeDtypeStruct((t_tokens, 2 * CONV_WIDTH), F32)],
        scratch_shapes=[pltpu.VMEM((TM_PROJ, D_MODEL), BF16),
                        pltpu.VMEM((TM_PROJ, HEAD_DIM), F32),
                        pltpu.VMEM((TM_PROJ, HEAD_DIM), F32)],
        compiler_params=_cparams(2),
        name="inproj",
    )(x2, pos2, g, w_bf, qn, kn, invf2)


def _attn_body(q_ref, ko_ref, kp_ref, vo_ref, vp_ref, o_ref, qf, kf, vf, m_sc, l_sc, acc_sc):
    c = pl.program_id(2)
    ch = ATTN_CHUNK
    qf[...] = q_ref[0, 0].astype(F32)
    kf[:ch] = kp_ref[0, 0].astype(F32)
    kf[ch:] = ko_ref[0, 0].astype(F32)
    vf[:ch] = vp_ref[0, 0].astype(F32)
    vf[ch:] = vo_ref[0, 0].astype(F32)

    qi = lax.broadcasted_iota(I32, (ATTN_BLOCK, 2 * ATTN_BLOCK), 0)
    kj = lax.broadcasted_iota(I32, (ATTN_BLOCK, 2 * ATTN_BLOCK), 1)
    band = (kj >= qi) & (kj <= qi + ATTN_BLOCK)
    bias_all = jnp.where(band, 0.0, NEG)
    bias_own = jnp.where(band & (kj >= ATTN_BLOCK), 0.0, NEG)

    def rows(start, n, d):
        return pl.ds(start, n) if d == 1 else pl.ds(start, n, stride=d)

    def block(t, d, p_idx):
        unit = ATTN_BLOCK * d
        qs = (t // d) * unit + (t % d)
        ks = ch + qs - unit
        q = qf[rows(qs, ATTN_BLOCK, d), :].astype(BF16)
        kc = kf[rows(ks, 2 * ATTN_BLOCK, d), :].astype(BF16)
        vc = vf[rows(ks, 2 * ATTN_BLOCK, d), :].astype(BF16)
        s = lax.dot_general(q, kc, (((1,), (1,)), ((), ())), preferred_element_type=F32)
        has_prev = jnp.logical_or(c > 0, ks >= ch)
        s = s + jnp.where(has_prev, bias_all, bias_own)
        m_b = jnp.max(s, axis=-1, keepdims=True)
        p = jnp.exp(s - m_b)
        r = rows(qs, ATTN_BLOCK, d)
        m_sc[p_idx, r, :] = m_b
        l_sc[p_idx, r, :] = jnp.sum(p, axis=-1, keepdims=True)
        acc_sc[p_idx, r, :] = jnp.dot(p.astype(BF16), vc, preferred_element_type=F32)

    for p_idx, d in enumerate(DILATIONS):
        def step(t, carry, d=d, p_idx=p_idx):
            block(t, d, p_idx)
            return carry
        lax.fori_loop(0, ch // ATTN_BLOCK, step, 0, unroll=ATTN_UNROLL)

    def merge(i, carry):
        r = pl.ds(pl.multiple_of(i * MERGE_ROWS, MERGE_ROWS), MERGE_ROWS)
        ms = [m_sc[p_idx, r, :] for p_idx in range(len(DILATIONS))]
        m = functools.reduce(jnp.maximum, ms)
        num = jnp.zeros((MERGE_ROWS, HEAD_DIM), F32)
        den = jnp.zeros((MERGE_ROWS, 1), F32)
        for p_idx in range(len(DILATIONS)):
            a = jnp.exp(ms[p_idx] - m)
            num = num + a * acc_sc[p_idx, r, :]
            den = den + a * l_sc[p_idx, r, :]
        o_ref[0, 0, r, :] = (num / den).astype(BF16)
        return carry

    lax.fori_loop(0, ch // MERGE_ROWS, merge, 0)


def _attention(q, k, v, batch, seq):
    ch = ATTN_CHUNK
    own = pl.BlockSpec((1, 1, ch, HEAD_DIM), lambda b, h, c: (b, h, c, 0))
    prev = pl.BlockSpec((1, 1, ch, HEAD_DIM), lambda b, h, c: (b, h, jnp.maximum(c - 1, 0), 0))
    return pl.pallas_call(
        _attn_body,
        grid=(batch, N_HEADS, seq // ch),
        in_specs=[own, own, prev, own, prev],
        out_specs=own,
        out_shape=jax.ShapeDtypeStruct((batch, N_HEADS, seq, HEAD_DIM), BF16),
        scratch_shapes=[pltpu.VMEM((ch, HEAD_DIM), F32),
                        pltpu.VMEM((2 * ch, HEAD_DIM), F32),
                        pltpu.VMEM((2 * ch, HEAD_DIM), F32),
                        pltpu.VMEM((len(DILATIONS), ch, 1), F32),
                        pltpu.VMEM((len(DILATIONS), ch, 1), F32),
                        pltpu.VMEM((len(DILATIONS), ch, HEAD_DIM), F32)],
        compiler_params=_cparams(3),
        name="dilated_attn",
    )(q, k, k, v, v)


def _conv_body(um_ref, uh_ref, w_ref, b_ref, g_ref, bb_ref, o_ref, y_sc, z_sc):
    i = pl.program_id(1)
    cw = CONV_WIDTH
    uh = uh_ref[...]
    yh = uh[:, :cw] * jax.nn.sigmoid(uh[:, cw:])
    y_sc[:CONV_HALO] = jnp.where(i > 0, yh, 0.0)

    def glu(rc, carry):
        r0 = pl.multiple_of(rc * RC_CONV, RC_CONV)
        um = um_ref[pl.ds(r0, RC_CONV), :]
        y_sc[pl.ds(CONV_HALO + r0, RC_CONV), :] = um[:, :cw] * jax.nn.sigmoid(um[:, cw:])
        return carry

    lax.fori_loop(0, TC_CONV // RC_CONV, glu, 0)

    win_rows = RC_CONV + CONV_HALO
    first_off = CONV_HALO - (CONV_KERNEL - 1)

    def chunk(rc, carry):
        r0 = pl.multiple_of(rc * RC_CONV, RC_CONV)
        for cg in range(cw // CONV_COLS):
            cols = slice(cg * CONV_COLS, (cg + 1) * CONV_COLS)
            win = y_sc[pl.ds(r0, win_rows), cols]
            acc = jnp.zeros((RC_CONV, CONV_COLS), F32)
            for phase in range(SUBLANES):
                shifted = win if phase == 0 else pltpu.roll(win, win_rows - phase, 0)
                for k in range(CONV_KERNEL):
                    off = first_off + k
                    if off % SUBLANES == phase:
                        base = off - phase
                        acc = acc + shifted[base:base + RC_CONV] * w_ref[k:k + 1, cols]
            z_sc[:, cols] = acc
        yf = z_sc[...] + b_ref[...]
        mu = jnp.mean(yf, axis=-1, keepdims=True)
        yc = yf - mu
        var = jnp.mean(yc * yc, axis=-1, keepdims=True)
        yn = yc * lax.rsqrt(var + EPS) * g_ref[...] + bb_ref[...]
        o_ref[pl.ds(r0, RC_CONV), :] = (yn * jax.nn.sigmoid(yn)).astype(BF16)
        return carry

    lax.fori_loop(0, TC_CONV // RC_CONV, chunk, 0)


def _conv(u, conv_w, conv_b, ln_g, ln_b, batch, seq):
    nsb = seq // TC_CONV
    hpt = TC_CONV // CONV_HALO
    row = lambda: pl.BlockSpec((1, CONV_WIDTH), lambda b, i: (0, 0))
    return pl.pallas_call(
        _conv_body,
        grid=(batch, nsb),
        in_specs=[
            pl.BlockSpec((TC_CONV, 2 * CONV_WIDTH), lambda b, i: (b * nsb + i, 0)),
            pl.BlockSpec((CONV_HALO, 2 * CONV_WIDTH),
                         lambda b, i: (jnp.maximum((b * nsb + i) * hpt - 1, 0), 0)),
            pl.BlockSpec((CONV_KERNEL, CONV_WIDTH), lambda b, i: (0, 0)),
            row(), row(), row(),
        ],
        out_specs=pl.BlockSpec((TC_CONV, CONV_WIDTH), lambda b, i: (b * nsb + i, 0)),
        out_shape=jax.ShapeDtypeStruct((batch * seq, CONV_WIDTH), BF16),
        scratch_shapes=[pltpu.VMEM((CONV_HALO + TC_CONV, CONV_WIDTH), F32),
                        pltpu.VMEM((RC_CONV, CONV_WIDTH), F32)],
        compiler_params=_cparams(2),
        name="conformer_conv",
    )(u, u, conv_w, conv_b, ln_g, ln_b)


def _router(x1, fg_ref, wr_ref, br_ref, info_ref, wts_ref, cnt_ref, carry_sc):
    tm = x1.shape[0]

    @pl.when(pl.program_id(0) == 0)
    def _():
        carry_sc[...] = jnp.zeros_like(carry_sc)

    t = _rms(x1, fg_ref[...]).astype(BF16)
    logits = jnp.dot(t, wr_ref[...], preferred_element_type=F32) + br_ref[...]
    lane = lax.broadcasted_iota(I32, (tm, LANES), 1)
    lanef = lane.astype(F32)
    ninf = -jnp.inf

    def first_argmax(vals):
        top = jnp.max(vals, axis=-1, keepdims=True)
        idx = jnp.min(jnp.where(vals == top, lanef, float(LANES)), axis=-1, keepdims=True)
        return top, idx

    gl = jnp.where(lane < N_GROUPS, logits, ninf)
    gmax, gidx = first_argmax(gl)
    p_sel = 1.0 / jnp.sum(jnp.exp(gl - gmax), axis=-1, keepdims=True)
    lo = N_GROUPS + EXPERTS_PER_GROUP * gidx
    el = jnp.where((lanef >= lo) & (lanef < lo + EXPERTS_PER_GROUP), logits, ninf)
    v1, i1 = first_argmax(el)
    el2 = jnp.where(lanef == i1, ninf, el)
    v2, i2 = first_argmax(el2)
    e21 = jnp.exp(v2 - v1)
    w1 = p_sel / (1.0 + e21)
    w2 = p_sel * e21 / (1.0 + e21)

    hit1 = lanef == i1
    hit2 = lanef == i2
    mh = jnp.where(hit1 | hit2, 1.0, 0.0)
    ri = lax.broadcasted_iota(I32, (tm, tm), 0)
    ci = lax.broadcasted_iota(I32, (tm, tm), 1)
    tri = jnp.where(ci < ri, 1.0, 0.0).astype(BF16)
    before = jnp.dot(tri, mh.astype(BF16), preferred_element_type=F32) + carry_sc[...]
    rank1 = jnp.sum(jnp.where(hit1, before, 0.0), axis=-1, keepdims=True)
    rank2 = jnp.sum(jnp.where(hit2, before, 0.0), axis=-1, keepdims=True)
    carry_sc[...] = carry_sc[...] + jnp.sum(mh, axis=0, keepdims=True)
    cnt_ref[...] = carry_sc[...]

    info = jnp.where(lane == 0, i1 - N_GROUPS,
                     jnp.where(lane == 1, i2 - N_GROUPS,
                               jnp.where(lane == 2, rank1, jnp.where(lane == 3, rank2, 0.0))))
    info_ref[...] = info.astype(I32)
    wts_ref[...] = jnp.where(lane == 0, w1, jnp.where(lane == 1, w2, 0.0))


def _router_specs(tm):
    in_specs = [pl.BlockSpec((1, D_MODEL), lambda i: (0, 0)),
                pl.BlockSpec((D_MODEL, LANES), lambda i: (0, 0)),
                pl.BlockSpec((1, LANES), lambda i: (0, 0))]
    out_specs = [pl.BlockSpec((tm, LANES), lambda i: (i, 0)),
                 pl.BlockSpec((tm, LANES), lambda i: (i, 0)),
                 pl.BlockSpec((1, LANES), lambda i: (0, 0))]
    return in_specs, out_specs


def _router_shapes(t_tokens):
    return [jax.ShapeDtypeStruct((t_tokens, LANES), I32),
            jax.ShapeDtypeStruct((t_tokens, LANES), F32),
            jax.ShapeDtypeStruct((1, LANES), F32)]


def _outproj_body(attn_ref, conv_ref, x_ref, wo_ref, fg_ref, wr_ref, br_ref,
                  x1_ref, info_ref, wts_ref, cnt_ref, carry_sc):
    cat = jnp.concatenate([attn_ref[0, h] for h in range(N_HEADS)] + [conv_ref[...]], axis=-1)
    x1 = x_ref[...] + jnp.dot(cat, wo_ref[...], preferred_element_type=F32)
    x1_ref[...] = x1
    _router(x1, fg_ref, wr_ref, br_ref, info_ref, wts_ref, cnt_ref, carry_sc)


def _outproj_router(attn, conv, x2, wo_bf, fg, wr_bf, br, batch, seq):
    t_tokens = batch * seq
    tm = TM_OUT
    nsb = seq // tm
    r_in, r_out = _router_specs(tm)
    return pl.pallas_call(
        _outproj_body,
        grid=(t_tokens // tm,),
        in_specs=[pl.BlockSpec((1, N_HEADS, tm, HEAD_DIM), lambda i: (i // nsb, 0, i % nsb, 0)),
                  pl.BlockSpec((tm, CONV_WIDTH), lambda i: (i, 0)),
                  pl.BlockSpec((tm, D_MODEL), lambda i: (i, 0)),
                  pl.BlockSpec((D_MODEL, D_MODEL), lambda i: (0, 0))] + r_in,
        out_specs=[pl.BlockSpec((tm, D_MODEL), lambda i: (i, 0))] + r_out,
        out_shape=[jax.ShapeDtypeStruct((t_tokens, D_MODEL), F32)] + _router_shapes(t_tokens),
        scratch_shapes=[pltpu.VMEM((1, LANES), F32)],
        compiler_params=_cparams(1),
        name="outproj_router",
    )(attn, conv, x2, wo_bf, fg, wr_bf, br)


def _pool_body(x_ref, xh_ref, og_ref, pw_ref, pb_ref, ps_ref, fg_ref, wr_ref, br_ref,
               x1_ref, info_ref, wts_ref, cnt_ref, carry_sc, h_sc, *, tiles_per_seq):
    tm = x_ref.shape[0]
    ts = pl.program_id(0) % tiles_per_seq
    x = x_ref[...]
    h_sc[:POOL_HALO] = jnp.where(ts > 0, _rms(xh_ref[...], og_ref[...]), 0.0)
    h_sc[POOL_HALO:] = _rms(x, og_ref[...])
    tpos = ts * tm + lax.broadcasted_iota(I32, (tm, 1), 0) + 1
    ys = []
    for g, w in enumerate(POOL_WINDOWS):
        cols = slice(g * POOL_GROUP, (g + 1) * POOL_GROUP)
        cur = h_sc[POOL_HALO:, cols]
        tot = cur
        for back in range(1, w):
            tot = tot + h_sc[POOL_HALO - back:POOL_HALO - back + tm, cols]
        cnt = jnp.minimum(tpos, w).astype(F32)
        mixed = tot / cnt - cur
        ys.append(jnp.dot(mixed.astype(BF16), pw_ref[g], preferred_element_type=F32))
    y = jnp.concatenate(ys, axis=-1)
    x1 = x + (y + pb_ref[...]) * ps_ref[...]
    x1_ref[...] = x1
    _router(x1, fg_ref, wr_ref, br_ref, info_ref, wts_ref, cnt_ref, carry_sc)


def _pool_router(x2, og, pw_bf, pb, ps, fg, wr_bf, br, batch, seq):
    t_tokens = batch * seq
    tm = TM_OUT
    hpt = tm // POOL_HALO
    r_in, r_out = _router_specs(tm)
    vec = lambda: pl.BlockSpec((1, D_MODEL), lambda i: (0, 0))
    return pl.pallas_call(
        functools.partial(_pool_body, tiles_per_seq=seq // tm),
        grid=(t_tokens // tm,),
        in_specs=[pl.BlockSpec((tm, D_MODEL), lambda i: (i, 0)),
                  pl.BlockSpec((POOL_HALO, D_MODEL), lambda i: (jnp.maximum(i * hpt - 1, 0), 0)),
                  vec(),
                  pl.BlockSpec((len(POOL_WINDOWS), POOL_GROUP, POOL_GROUP), lambda i: (0, 0, 0)),
                  vec(), vec()] + r_in,
        out_specs=[pl.BlockSpec((tm, D_MODEL), lambda i: (i, 0))] + r_out,
        out_shape=[jax.ShapeDtypeStruct((t_tokens, D_MODEL), F32)] + _router_shapes(t_tokens),
        scratch_shapes=[pltpu.VMEM((1, LANES), F32),
                        pltpu.VMEM((POOL_HALO + tm, D_MODEL), F32)],
        compiler_params=_cparams(1),
        name="pool_router",
    )(x2, x2, og, pw_bf, pb, ps, fg, wr_bf, br)


def _bulk_wait(src, dst, sem):
    pltpu.make_async_copy(src, dst, sem).wait()


def _dispatch_body(pos_ref, x_ref, xs_hbm, sem):
    def issue(r, carry):
        for k in range(2):
            pltpu.make_async_copy(x_ref.at[pl.ds(r, 1)],
                                  xs_hbm.at[pl.ds(pos_ref[0, 0, 2 * r + k], 1)], sem).start()
        return carry

    lax.fori_loop(0, TD_DISP, issue, 0, unroll=DMA_UNROLL)
    for k in range(2):
        _bulk_wait(x_ref, xs_hbm.at[pl.ds(0, TD_DISP)], sem)


def _dispatch(pos3, x1, n_rows):
    t_tokens = x1.shape[0]
    return pl.pallas_call(
        _dispatch_body,
        grid=(t_tokens // TD_DISP,),
        in_specs=[pl.BlockSpec((1, 1, 2 * TD_DISP), lambda i: (i, 0, 0), memory_space=pltpu.SMEM),
                  pl.BlockSpec((TD_DISP, D_MODEL), lambda i: (i, 0))],
        out_specs=pl.BlockSpec(memory_space=pl.ANY),
        out_shape=jax.ShapeDtypeStruct((n_rows, D_MODEL), F32),
        scratch_shapes=[pltpu.SemaphoreType.DMA(())],
        compiler_params=_cparams(1),
        name="moe_dispatch",
    )(pos3, x1)


def _expert_body(te_ref, nu_ref, xs_ref, fg_ref, wg_ref, wu_ref, wd_ref, ys_ref, wg_sc, wu_sc, wd_sc):
    j = pl.program_id(0)

    @pl.when(j < nu_ref[0])
    def _():
        prev = te_ref[jnp.maximum(j - 1, 0)]

        @pl.when(jnp.logical_or(j == 0, te_ref[j] != prev))
        def _():
            wg_sc[...] = wg_ref[0, 0].astype(BF16)
            wu_sc[...] = wu_ref[0, 0].astype(BF16)
            wd_sc[...] = wd_ref[0, 0].astype(BF16)

        t = _rms(xs_ref[...], fg_ref[...]).astype(BF16)
        a = jnp.dot(t, wg_sc[...], preferred_element_type=F32)
        b = jnp.dot(t, wu_sc[...], preferred_element_type=F32)
        hid = (a * jax.nn.sigmoid(a) * b).astype(BF16)
        ys_ref[...] = jnp.dot(hid, wd_sc[...], preferred_element_type=F32)


def _experts(tile_expert, n_used, xs, fg, wg, wu, wd, layer):
    n_rows = xs.shape[0]
    n_tiles = n_rows // TM_EXP
    row_idx = lambda j, te, nu: (jnp.minimum(j, nu[0] - 1), 0)
    return pl.pallas_call(
        _expert_body,
        grid_spec=pltpu.PrefetchScalarGridSpec(
            num_scalar_prefetch=2,
            grid=(n_tiles,),
            in_specs=[pl.BlockSpec((TM_EXP, D_MODEL), row_idx),
                      pl.BlockSpec((1, D_MODEL), lambda j, te, nu: (0, 0)),
                      pl.BlockSpec((1, 1, D_MODEL, EXPERT_HIDDEN), lambda j, te, nu: (layer, te[j], 0, 0)),
                      pl.BlockSpec((1, 1, D_MODEL, EXPERT_HIDDEN), lambda j, te, nu: (layer, te[j], 0, 0)),
                      pl.BlockSpec((1, 1, EXPERT_HIDDEN, D_MODEL), lambda j, te, nu: (layer, te[j], 0, 0))],
            out_specs=pl.BlockSpec((TM_EXP, D_MODEL), row_idx),
            scratch_shapes=[pltpu.VMEM((D_MODEL, EXPERT_HIDDEN), BF16),
                            pltpu.VMEM((D_MODEL, EXPERT_HIDDEN), BF16),
                            pltpu.VMEM((EXPERT_HIDDEN, D_MODEL), BF16)]),
        out_shape=jax.ShapeDtypeStruct((n_rows, D_MODEL), F32),
        compiler_params=_cparams(1),
        name="moe_experts",
    )(tile_expert, n_used, xs, fg, wg, wu, wd)


def _combine_body(pos_ref, posn_ref, x_ref, wts_ref, ys_hbm, o_ref, gbuf, sems, *, n_steps):
    i = pl.program_id(0)
    slot = i % 2

    def issue(p_ref, s):
        def body(r, carry):
            for k in range(2):
                pltpu.make_async_copy(ys_hbm.at[pl.ds(p_ref[0, 0, 2 * r + k], 1)],
                                      gbuf.at[s, k, pl.ds(r, 1)], sems.at[s]).start()
            return carry
        lax.fori_loop(0, TC_COMB, body, 0, unroll=DMA_UNROLL)

    @pl.when(i == 0)
    def _():
        issue(pos_ref, 0)

    @pl.when(i + 1 < n_steps)
    def _():
        issue(posn_ref, 1 - slot)

    for k in range(2):
        _bulk_wait(ys_hbm.at[pl.ds(0, TC_COMB)], gbuf.at[slot, k], sems.at[slot])
    w = wts_ref[...]
    o_ref[...] = x_ref[...] + w[:, 0:1] * gbuf[slot, 0] + w[:, 1:2] * gbuf[slot, 1]


def _combine(pos3, x1, wts, ys):
    t_tokens = x1.shape[0]
    n_steps = t_tokens // TC_COMB
    return pl.pallas_call(
        functools.partial(_combine_body, n_steps=n_steps),
        grid=(n_steps,),
        in_specs=[pl.BlockSpec((1, 1, 2 * TC_COMB), lambda i: (i, 0, 0), memory_space=pltpu.SMEM),
                  pl.BlockSpec((1, 1, 2 * TC_COMB), lambda i: (jnp.minimum(i + 1, n_steps - 1), 0, 0),
                               memory_space=pltpu.SMEM),
                  pl.BlockSpec((TC_COMB, D_MODEL), lambda i: (i, 0)),
                  pl.BlockSpec((TC_COMB, LANES), lambda i: (i, 0)),
                  pl.BlockSpec(memory_space=pl.ANY)],
        out_specs=pl.BlockSpec((TC_COMB, D_MODEL), lambda i: (i, 0)),
        out_shape=jax.ShapeDtypeStruct((t_tokens, D_MODEL), F32),
        scratch_shapes=[pltpu.VMEM((2, 2, TC_COMB, D_MODEL), F32),
                        pltpu.SemaphoreType.DMA((2,))],
        compiler_params=_cparams(1),
        name="moe_combine",
    )(pos3, pos3, x1, wts, ys)


def _moe(x1, info, wts, counts, fg, wg, wu, wd, layer):
    t_tokens = x1.shape[0]
    n_tiles = 2 * t_tokens // TM_EXP + N_EXPERTS
    cnt = counts[0, N_GROUPS:N_GROUPS + N_EXPERTS].astype(I32)
    padded = (cnt + TM_EXP - 1) // TM_EXP * TM_EXP
    seg_end = jnp.cumsum(padded)
    seg_start = seg_end - padded
    n_used = seg_end[-1] // TM_EXP
    tile_row = jnp.minimum(jnp.arange(n_tiles, dtype=I32), n_used - 1) * TM_EXP
    tile_expert = jnp.sum((tile_row[:, None] >= seg_end[None, :]).astype(I32), axis=1)
    eid = info[:, 0:2]
    start_of = jnp.sum(jnp.where(eid[:, :, None] == jnp.arange(N_EXPERTS, dtype=I32), seg_start, 0), axis=-1)
    pos = start_of + info[:, 2:4]
    xs = _dispatch(pos.reshape(t_tokens // TD_DISP, 1, 2 * TD_DISP), x1, n_tiles * TM_EXP)
    ys = _experts(tile_expert, n_used.reshape(1), xs, fg, wg, wu, wd, layer)
    return _combine(pos.reshape(t_tokens // TC_COMB, 1, 2 * TC_COMB), x1, wts, ys)


def _router_params(w_rg, b_rg, w_re, b_re):
    pad = LANES - N_GROUPS - N_EXPERTS
    wr = jnp.concatenate([w_rg, w_re, jnp.zeros((D_MODEL, pad), F32)], axis=1).astype(BF16)
    br = jnp.concatenate([b_rg, b_re, jnp.zeros((pad,), F32)]).reshape(1, LANES)
    return wr, br


def kernel(x, positions, even_norm, w_in, q_norm, k_norm, conv_w, conv_b, conv_ln_g, conv_ln_b, w_out,
           odd_norm, pool_w, pool_b, pool_scale, ffn_norm, w_router_group, b_router_group,
           w_router_expert, b_router_expert, w_expert_gate, w_expert_up, w_expert_down):
    batch, seq, d = x.shape
    depth = ffn_norm.shape[0]
    assert d == D_MODEL and seq % ATTN_CHUNK == 0 and seq % TM_PROJ == 0
    t_tokens = batch * seq
    half = HEAD_DIM // 2
    inv_freq = jnp.float32(ROPE_THETA) ** (-jnp.arange(half, dtype=F32) / half)
    invf2 = jnp.concatenate([inv_freq, inv_freq]).reshape(1, HEAD_DIM)
    pos2 = positions.reshape(t_tokens, 1)
    row = lambda a: a.reshape(1, -1)

    x2 = x.reshape(t_tokens, d)
    for i in range(depth):
        j = i // 2
        wr, br = _router_params(w_router_group[i], b_router_group[i], w_router_expert[i], b_router_expert[i])
        fg = row(ffn_norm[i])
        if i % 2 == 0:
            q, k, v, u = _inproj(x2, pos2, row(even_norm[j]), w_in[j].astype(BF16), row(q_norm[j]),
                                 row(k_norm[j]), invf2, batch, seq)
            attn = _attention(q, k, v, batch, seq)
            conv = _conv(u, conv_w[j], row(conv_b[j]), row(conv_ln_g[j]), row(conv_ln_b[j]), batch, seq)
            x1, info, wts, counts = _outproj_router(attn, conv, x2, w_out[j].astype(BF16), fg, wr, br,
                                                    batch, seq)
        else:
            x1, info, wts, counts = _pool_router(x2, row(odd_norm[j]), pool_w[j].astype(BF16),
                                                 row(pool_b[j]), row(pool_scale[j]), fg, wr, br, batch, seq)
        x2 = _moe(x1, info, wts, counts, fg, w_expert_gate, w_expert_up, w_expert_down, i)
    return x2.reshape(batch, seq, d)
```

```python
import functools

import jax
import jax.numpy as jnp
from jax import lax
from jax.experimental import pallas as pl
from jax.experimental.pallas import tpu as pltpu

F32 = jnp.float32
BF16 = jnp.bfloat16
I32 = jnp.int32

D_MODEL = 2048
HEAD_DIM = 128
N_HEADS = 8
ATTN_WIDTH = N_HEADS * HEAD_DIM
ATTN_BLOCK = 128
DILATIONS = (1, 4, 16)
ATTN_CHUNK = ATTN_BLOCK * max(DILATIONS)
ATTN_UNROLL = 8
MERGE_ROWS = 256
CONV_WIDTH = D_MODEL - ATTN_WIDTH
CONV_KERNEL = 31
CONV_HALO = 32
IN_PROJ_WIDTH = 3 * ATTN_WIDTH + 2 * CONV_WIDTH
POOL_WINDOWS = (2, 4, 8, 16)
POOL_GROUP = D_MODEL // len(POOL_WINDOWS)
POOL_HALO = 16
N_GROUPS = 4
EXPERTS_PER_GROUP = 8
N_EXPERTS = N_GROUPS * EXPERTS_PER_GROUP
EXPERT_HIDDEN = 512
ROPE_THETA = 10000.0
EPS = 1e-6
LANES = 128
NEG = -1e30

TM_PROJ = 256
TN_PROJ = 1024
TM_OUT = 512
ROUTER_ROWS = 256
TC_CONV = 512
RC_CONV = 32
CONV_COLS = 256
SUBLANES = 8
TM_EXP = 256
TD_DISP = 512
TC_COMB = 256
DMA_UNROLL = 8
VMEM_LIMIT = 56 * 1024 * 1024


def _cparams(n_axes):
    return pltpu.CompilerParams(dimension_semantics=("arbitrary",) * n_axes,
                                vmem_limit_bytes=VMEM_LIMIT)


def _rms(x, g):
    return x * lax.rsqrt(jnp.mean(x * x, axis=-1, keepdims=True) + EPS) * g


def _inproj_body(x_ref, pos_ref, g_ref, w_ref, qn_ref, kn_ref, invf_ref, q_ref, k_ref, v_ref, u_ref, hn_sc):
    hn_sc[...] = _rms(x_ref[...], g_ref[...]).astype(BF16)
    ang = pos_ref[...].astype(F32) * invf_ref[...]
    lane = lax.broadcasted_iota(I32, ang.shape, 1)
    cs = jnp.cos(ang)
    sn = jnp.where(lane < HEAD_DIM // 2, -1.0, 1.0) * jnp.sin(ang)

    def section(n):
        return jnp.dot(hn_sc[...], w_ref[:, n * TN_PROJ:(n + 1) * TN_PROJ], preferred_element_type=F32)

    def qk_heads(acc, norm_ref, out_ref, scale):
        for h in range(N_HEADS):
            y = _rms(acc[:, h * HEAD_DIM:(h + 1) * HEAD_DIM], norm_ref[...])
            y = y * cs + pltpu.roll(y, HEAD_DIM // 2, 1) * sn
            out_ref[0, h] = (y * scale).astype(BF16)

    qk_heads(section(0), qn_ref, q_ref, HEAD_DIM ** -0.5)
    qk_heads(section(1), kn_ref, k_ref, 1.0)
    acc = section(2)
    for h in range(N_HEADS):
        v_ref[0, h] = acc[:, h * HEAD_DIM:(h + 1) * HEAD_DIM].astype(BF16)
    u_ref[:, :TN_PROJ] = section(3)
    u_ref[:, TN_PROJ:] = section(4)


def _inproj(x2, pos2, g, w_bf, qn, kn, invf2, batch, seq):
    t_tokens = batch * seq
    nsb = seq // TM_PROJ
    head_spec = pl.BlockSpec((1, N_HEADS, TM_PROJ, HEAD_DIM), lambda i: (i // nsb, 0, i % nsb, 0))
    head_shape = jax.ShapeDtypeStruct((batch, N_HEADS, seq, HEAD_DIM), BF16)
    return pl.pallas_call(
        _inproj_body,
        grid=(t_tokens // TM_PROJ,),
        in_specs=[
            pl.BlockSpec((TM_PROJ, D_MODEL), lambda i: (i, 0)),
            pl.BlockSpec((TM_PROJ, 1), lambda i: (i, 0)),
            pl.BlockSpec((1, D_MODEL), lambda i: (0, 0)),
            pl.BlockSpec((D_MODEL, IN_PROJ_WIDTH), lambda i: (0, 0), pipeline_mode=pl.Buffered(1)),
            pl.BlockSpec((1, HEAD_DIM), lambda i: (0, 0)),
            pl.BlockSpec((1, HEAD_DIM), lambda i: (0, 0)),
            pl.BlockSpec((1, HEAD_DIM), lambda i: (0, 0)),
        ],
        out_specs=[head_spec, head_spec, head_spec,
                   pl.BlockSpec((TM_PROJ, 2 * CONV_WIDTH), lambda i: (i, 0))],
        out_shape=[head_shape, head_shape, head_shape,
                   jax.ShapeDtypeStruct((t_tokens, 2 * CONV_WIDTH), F32)],
        scratch_shapes=[pltpu.VMEM((TM_PROJ, D_MODEL), BF16)],
        compiler_params=_cparams(1),
        name="inproj",
    )(x2, pos2, g, w_bf, qn, kn, invf2)


def _attn_body(q_ref, ko_ref, kp_ref, vo_ref, vp_ref, o_ref, qf, kf, vf, m_sc, l_sc, acc_sc):
    c = pl.program_id(2)
    ch = ATTN_CHUNK
    qf[...] = q_ref[0, 0].astype(F32)
    kf[:ch] = kp_ref[0, 0].astype(F32)
    kf[ch:] = ko_ref[0, 0].astype(F32)
    vf[:ch] = vp_ref[0, 0].astype(F32)
    vf[ch:] = vo_ref[0, 0].astype(F32)

    qi = lax.broadcasted_iota(I32, (ATTN_BLOCK, 2 * ATTN_BLOCK), 0)
    kj = lax.broadcasted_iota(I32, (ATTN_BLOCK, 2 * ATTN_BLOCK), 1)
    band = (kj >= qi) & (kj <= qi + ATTN_BLOCK)
    bias_all = jnp.where(band, 0.0, NEG)
    bias_own = jnp.where(band & (kj >= ATTN_BLOCK), 0.0, NEG)

    def rows(start, n, d):
        return pl.ds(start, n) if d == 1 else pl.ds(start, n, stride=d)

    def block(t, d, p_idx):
        unit = ATTN_BLOCK * d
        qs = (t // d) * unit + (t % d)
        ks = ch + qs - unit
        q = qf[rows(qs, ATTN_BLOCK, d), :].astype(BF16)
        kc = kf[rows(ks, 2 * ATTN_BLOCK, d), :].astype(BF16)
        vc = vf[rows(ks, 2 * ATTN_BLOCK, d), :].astype(BF16)
        s = lax.dot_general(q, kc, (((1,), (1,)), ((), ())), preferred_element_type=F32)
        has_prev = jnp.logical_or(c > 0, ks >= ch)
        s = s + jnp.where(has_prev, bias_all, bias_own)
        m_b = jnp.max(s, axis=-1, keepdims=True)
        p = jnp.exp(s - m_b)
        r = rows(qs, ATTN_BLOCK, d)
        m_sc[p_idx, r, :] = m_b
        l_sc[p_idx, r, :] = jnp.sum(p, axis=-1, keepdims=True)
        acc_sc[p_idx, r, :] = jnp.dot(p.astype(BF16), vc, preferred_element_type=F32)

    for p_idx, d in enumerate(DILATIONS):
        def step(t, carry, d=d, p_idx=p_idx):
            block(t, d, p_idx)
            return carry
        lax.fori_loop(0, ch // ATTN_BLOCK, step, 0, unroll=ATTN_UNROLL)

    def merge(i, carry):
        r = pl.ds(pl.multiple_of(i * MERGE_ROWS, MERGE_ROWS), MERGE_ROWS)
        ms = [m_sc[p_idx, r, :] for p_idx in range(len(DILATIONS))]
        m = functools.reduce(jnp.maximum, ms)
        num = jnp.zeros((MERGE_ROWS, HEAD_DIM), F32)
        den = jnp.zeros((MERGE_ROWS, 1), F32)
        for p_idx in range(len(DILATIONS)):
            a = jnp.exp(ms[p_idx] - m)
            num = num + a * acc_sc[p_idx, r, :]
            den = den + a * l_sc[p_idx, r, :]
        o_ref[0, 0, r, :] = (num / den).astype(BF16)
        return carry

    lax.fori_loop(0, ch // MERGE_ROWS, merge, 0)


def _attention(q, k, v, batch, seq):
    ch = ATTN_CHUNK
    own = pl.BlockSpec((1, 1, ch, HEAD_DIM), lambda b, h, c: (b, h, c, 0))
    prev = pl.BlockSpec((1, 1, ch, HEAD_DIM), lambda b, h, c: (b, h, jnp.maximum(c - 1, 0), 0))
    return pl.pallas_call(
        _attn_body,
        grid=(batch, N_HEADS, seq // ch),
        in_specs=[own, own, prev, own, prev],
        out_specs=own,
        out_shape=jax.ShapeDtypeStruct((batch, N_HEADS, seq, HEAD_DIM), BF16),
        scratch_shapes=[pltpu.VMEM((ch, HEAD_DIM), F32),
                        pltpu.VMEM((2 * ch, HEAD_DIM), F32),
                        pltpu.VMEM((2 * ch, HEAD_DIM), F32),
                        pltpu.VMEM((len(DILATIONS), ch, 1), F32),
                        pltpu.VMEM((len(DILATIONS), ch, 1), F32),
                        pltpu.VMEM((len(DILATIONS), ch, HEAD_DIM), F32)],
        compiler_params=_cparams(3),
        name="dilated_attn",
    )(q, k, k, v, v)


def _conv_body(um_ref, uh_ref, w_ref, b_ref, g_ref, bb_ref, o_ref, y_sc, z_sc):
    i = pl.program_id(1)
    cw = CONV_WIDTH
    uh = uh_ref[...]
    yh = uh[:, :cw] * jax.nn.sigmoid(uh[:, cw:])
    y_sc[:CONV_HALO] = jnp.where(i > 0, yh, 0.0)

    def glu(rc, carry):
        r0 = pl.multiple_of(rc * RC_CONV, RC_CONV)
        um = um_ref[pl.ds(r0, RC_CONV), :]
        y_sc[pl.ds(CONV_HALO + r0, RC_CONV), :] = um[:, :cw] * jax.nn.sigmoid(um[:, cw:])
        return carry

    lax.fori_loop(0, TC_CONV // RC_CONV, glu, 0)

    win_rows = RC_CONV + CONV_HALO
    first_off = CONV_HALO - (CONV_KERNEL - 1)

    def chunk(rc, carry):
        r0 = pl.multiple_of(rc * RC_CONV, RC_CONV)
        for cg in range(cw // CONV_COLS):
            cols = slice(cg * CONV_COLS, (cg + 1) * CONV_COLS)
            win = y_sc[pl.ds(r0, win_rows), cols]
            acc = jnp.zeros((RC_CONV, CONV_COLS), F32)
            for phase in range(SUBLANES):
                shifted = win if phase == 0 else pltpu.roll(win, win_rows - phase, 0)
                for k in range(CONV_KERNEL):
                    off = first_off + k
                    if off % SUBLANES == phase:
                        base = off - phase
                        acc = acc + shifted[base:base + RC_CONV] * w_ref[k:k + 1, cols]
            z_sc[:, cols] = acc
        yf = z_sc[...] + b_ref[...]
        mu = jnp.mean(yf, axis=-1, keepdims=True)
        yc = yf - mu
        var = jnp.mean(yc * yc, axis=-1, keepdims=True)
        yn = yc * lax.rsqrt(var + EPS) * g_ref[...] + bb_ref[...]
        o_ref[pl.ds(r0, RC_CONV), :] = (yn * jax.nn.sigmoid(yn)).astype(BF16)
        return carry

    lax.fori_loop(0, TC_CONV // RC_CONV, chunk, 0)


def _conv(u, conv_w, conv_b, ln_g, ln_b, batch, seq):
    nsb = seq // TC_CONV
    hpt = TC_CONV // CONV_HALO
    row = lambda: pl.BlockSpec((1, CONV_WIDTH), lambda b, i: (0, 0))
    return pl.pallas_call(
        _conv_body,
        grid=(batch, nsb),
        in_specs=[
            pl.BlockSpec((TC_CONV, 2 * CONV_WIDTH), lambda b, i: (b * nsb + i, 0)),
            pl.BlockSpec((CONV_HALO, 2 * CONV_WIDTH),
                         lambda b, i: (jnp.maximum((b * nsb + i) * hpt - 1, 0), 0)),
            pl.BlockSpec((CONV_KERNEL, CONV_WIDTH), lambda b, i: (0, 0)),
            row(), row(), row(),
        ],
        out_specs=pl.BlockSpec((TC_CONV, CONV_WIDTH), lambda b, i: (b * nsb + i, 0)),
        out_shape=jax.ShapeDtypeStruct((batch * seq, CONV_WIDTH), BF16),
        scratch_shapes=[pltpu.VMEM((CONV_HALO + TC_CONV, CONV_WIDTH), F32),
                        pltpu.VMEM((RC_CONV, CONV_WIDTH), F32)],
        compiler_params=_cparams(2),
        name="conformer_conv",
    )(u, u, conv_w, conv_b, ln_g, ln_b)


def _router(x1, rows, fg_ref, wr_ref, br_ref, info_ref, wts_ref, cnt_ref, carry_sc):
    tm = x1.shape[0]
    t = _rms(x1, fg_ref[...]).astype(BF16)
    logits = jnp.dot(t, wr_ref[...], preferred_element_type=F32) + br_ref[...]
    lane = lax.broadcasted_iota(I32, (tm, LANES), 1)
    lanef = lane.astype(F32)
    ninf = -jnp.inf

    def first_argmax(vals):
        top = jnp.max(vals, axis=-1, keepdims=True)
        idx = jnp.min(jnp.where(vals == top, lanef, float(LANES)), axis=-1, keepdims=True)
        return top, idx

    gl = jnp.where(lane < N_GROUPS, logits, ninf)
    gmax, gidx = first_argmax(gl)
    p_sel = 1.0 / jnp.sum(jnp.exp(gl - gmax), axis=-1, keepdims=True)
    lo = N_GROUPS + EXPERTS_PER_GROUP * gidx
    el = jnp.where((lanef >= lo) & (lanef < lo + EXPERTS_PER_GROUP), logits, ninf)
    v1, i1 = first_argmax(el)
    el2 = jnp.where(lanef == i1, ninf, el)
    v2, i2 = first_argmax(el2)
    e21 = jnp.exp(v2 - v1)
    w1 = p_sel / (1.0 + e21)
    w2 = p_sel * e21 / (1.0 + e21)

    hit1 = lanef == i1
    hit2 = lanef == i2
    mh = jnp.where(hit1 | hit2, 1.0, 0.0)
    ri = lax.broadcasted_iota(I32, (tm, tm), 0)
    ci = lax.broadcasted_iota(I32, (tm, tm), 1)
    tri = jnp.where(ci < ri, 1.0, 0.0).astype(BF16)
    before = jnp.dot(tri, mh.astype(BF16), preferred_element_type=F32) + carry_sc[...]
    rank1 = jnp.sum(jnp.where(hit1, before, 0.0), axis=-1, keepdims=True)
    rank2 = jnp.sum(jnp.where(hit2, before, 0.0), axis=-1, keepdims=True)
    carry_sc[...] = carry_sc[...] + jnp.sum(mh, axis=0, keepdims=True)
    cnt_ref[...] = carry_sc[...]

    info = jnp.where(lane == 0, i1 - N_GROUPS,
                     jnp.where(lane == 1, i2 - N_GROUPS,
                               jnp.where(lane == 2, rank1, jnp.where(lane == 3, rank2, 0.0))))
    info_ref[rows, :] = info.astype(I32)
    wts_ref[rows, :] = jnp.where(lane == 0, w1, jnp.where(lane == 1, w2, 0.0))


def _router_specs(tm):
    in_specs = [pl.BlockSpec((1, D_MODEL), lambda i: (0, 0)),
                pl.BlockSpec((D_MODEL, LANES), lambda i: (0, 0)),
                pl.BlockSpec((1, LANES), lambda i: (0, 0))]
    out_specs = [pl.BlockSpec((tm, LANES), lambda i: (i, 0)),
                 pl.BlockSpec((tm, LANES), lambda i: (i, 0)),
                 pl.BlockSpec((1, LANES), lambda i: (0, 0))]
    return in_specs, out_specs


def _router_shapes(t_tokens):
    return [jax.ShapeDtypeStruct((t_tokens, LANES), I32),
            jax.ShapeDtypeStruct((t_tokens, LANES), F32),
            jax.ShapeDtypeStruct((1, LANES), F32)]


def _zero_carry_on_first_step(carry_sc):
    @pl.when(pl.program_id(0) == 0)
    def _():
        carry_sc[...] = jnp.zeros_like(carry_sc)


def _outproj_body(attn_ref, conv_ref, x_ref, wo_ref, fg_ref, wr_ref, br_ref,
                  x1_ref, info_ref, wts_ref, cnt_ref, carry_sc):
    _zero_carry_on_first_step(carry_sc)
    for half in range(TM_OUT // ROUTER_ROWS):
        rows = pl.ds(half * ROUTER_ROWS, ROUTER_ROWS)
        cat = jnp.concatenate([attn_ref[0, h, rows, :] for h in range(N_HEADS)] + [conv_ref[rows, :]], axis=-1)
        x1 = x_ref[rows, :] + jnp.dot(cat, wo_ref[...], preferred_element_type=F32)
        x1_ref[rows, :] = x1
        _router(x1, rows, fg_ref, wr_ref, br_ref, info_ref, wts_ref, cnt_ref, carry_sc)


def _outproj_router(attn, conv, x2, wo_bf, fg, wr_bf, br, batch, seq):
    t_tokens = batch * seq
    tm = TM_OUT
    nsb = seq // tm
    r_in, r_out = _router_specs(tm)
    return pl.pallas_call(
        _outproj_body,
        grid=(t_tokens // tm,),
        in_specs=[pl.BlockSpec((1, N_HEADS, tm, HEAD_DIM), lambda i: (i // nsb, 0, i % nsb, 0)),
                  pl.BlockSpec((tm, CONV_WIDTH), lambda i: (i, 0)),
                  pl.BlockSpec((tm, D_MODEL), lambda i: (i, 0)),
                  pl.BlockSpec((D_MODEL, D_MODEL), lambda i: (0, 0))] + r_in,
        out_specs=[pl.BlockSpec((tm, D_MODEL), lambda i: (i, 0))] + r_out,
        out_shape=[jax.ShapeDtypeStruct((t_tokens, D_MODEL), F32)] + _router_shapes(t_tokens),
        scratch_shapes=[pltpu.VMEM((1, LANES), F32)],
        compiler_params=_cparams(1),
        name="outproj_router",
    )(attn, conv, x2, wo_bf, fg, wr_bf, br)


def _pool_body(x_ref, xh_ref, og_ref, pw_ref, pb_ref, ps_ref, fg_ref, wr_ref, br_ref,
               x1_ref, info_ref, wts_ref, cnt_ref, carry_sc, h_sc, *, tiles_per_seq):
    tm = x_ref.shape[0]
    ts = pl.program_id(0) % tiles_per_seq
    _zero_carry_on_first_step(carry_sc)
    h_sc[:POOL_HALO] = jnp.where(ts > 0, _rms(xh_ref[...], og_ref[...]), 0.0)
    h_sc[POOL_HALO:] = _rms(x_ref[...], og_ref[...])
    for half in range(tm // ROUTER_ROWS):
        r0 = half * ROUTER_ROWS
        rows = pl.ds(r0, ROUTER_ROWS)
        tpos = ts * tm + r0 + lax.broadcasted_iota(I32, (ROUTER_ROWS, 1), 0) + 1
        ys = []
        for g, w in enumerate(POOL_WINDOWS):
            cols = slice(g * POOL_GROUP, (g + 1) * POOL_GROUP)
            ext = h_sc[r0:r0 + POOL_HALO + ROUTER_ROWS, cols]
            tot = ext
            span = 1
            while span < w:
                tot = tot + pltpu.roll(tot, span, 0)
                span *= 2
            cur = ext[POOL_HALO:]
            cnt = jnp.minimum(tpos, w).astype(F32)
            mixed = tot[POOL_HALO:] / cnt - cur
            ys.append(jnp.dot(mixed.astype(BF16), pw_ref[g], preferred_element_type=F32))
        y = jnp.concatenate(ys, axis=-1)
        x1 = x_ref[rows, :] + (y + pb_ref[...]) * ps_ref[...]
        x1_ref[rows, :] = x1
        _router(x1, rows, fg_ref, wr_ref, br_ref, info_ref, wts_ref, cnt_ref, carry_sc)


def _pool_router(x2, og, pw_bf, pb, ps, fg, wr_bf, br, batch, seq):
    t_tokens = batch * seq
    tm = TM_OUT
    hpt = tm // POOL_HALO
    r_in, r_out = _router_specs(tm)
    vec = lambda: pl.BlockSpec((1, D_MODEL), lambda i: (0, 0))
    return pl.pallas_call(
        functools.partial(_pool_body, tiles_per_seq=seq // tm),
        grid=(t_tokens // tm,),
        in_specs=[pl.BlockSpec((tm, D_MODEL), lambda i: (i, 0)),
                  pl.BlockSpec((POOL_HALO, D_MODEL), lambda i: (jnp.maximum(i * hpt - 1, 0), 0)),
                  vec(),
                  pl.BlockSpec((len(POOL_WINDOWS), POOL_GROUP, POOL_GROUP), lambda i: (0, 0, 0)),
                  vec(), vec()] + r_in,
        out_specs=[pl.BlockSpec((tm, D_MODEL), lambda i: (i, 0))] + r_out,
        out_shape=[jax.ShapeDtypeStruct((t_tokens, D_MODEL), F32)] + _router_shapes(t_tokens),
        scratch_shapes=[pltpu.VMEM((1, LANES), F32),
                        pltpu.VMEM((POOL_HALO + tm, D_MODEL), F32)],
        compiler_params=_cparams(1),
        name="pool_router",
    )(x2, x2, og, pw_bf, pb, ps, fg, wr_bf, br)


def _bulk_wait(src, dst, sem):
    pltpu.make_async_copy(src, dst, sem).wait()


def _dispatch_body(pos_ref, x_ref, xs_hbm, sem):
    def issue(r, carry):
        for k in range(2):
            pltpu.make_async_copy(x_ref.at[pl.ds(r, 1)],
                                  xs_hbm.at[pl.ds(pos_ref[0, 0, 2 * r + k], 1)], sem).start(priority=k)
        return carry

    lax.fori_loop(0, TD_DISP, issue, 0, unroll=DMA_UNROLL)
    for k in range(2):
        _bulk_wait(x_ref, xs_hbm.at[pl.ds(0, TD_DISP)], sem)


def _dispatch(pos3, x1, n_rows):
    t_tokens = x1.shape[0]
    return pl.pallas_call(
        _dispatch_body,
        grid=(t_tokens // TD_DISP,),
        in_specs=[pl.BlockSpec((1, 1, 2 * TD_DISP), lambda i: (i, 0, 0), memory_space=pltpu.SMEM),
                  pl.BlockSpec((TD_DISP, D_MODEL), lambda i: (i, 0))],
        out_specs=pl.BlockSpec(memory_space=pl.ANY),
        out_shape=jax.ShapeDtypeStruct((n_rows, D_MODEL), F32),
        scratch_shapes=[pltpu.SemaphoreType.DMA(())],
        compiler_params=_cparams(1),
        name="moe_dispatch",
    )(pos3, x1)


def _expert_body(te_ref, nu_ref, first_ref, nxt_ref, slot_ref, xs_ref, fg_ref, wg_hbm, wu_hbm, wd_hbm, ys_ref,
                 wg_f, wu_f, wd_f, wg_sc, wu_sc, wd_sc, sems, *, layer):
    j = pl.program_id(0)

    def weight_copies(e, s):
        return (pltpu.make_async_copy(wg_hbm.at[layer, e], wg_f.at[s], sems.at[s, 0]),
                pltpu.make_async_copy(wu_hbm.at[layer, e], wu_f.at[s], sems.at[s, 1]),
                pltpu.make_async_copy(wd_hbm.at[layer, e], wd_f.at[s], sems.at[s, 2]))

    @pl.when(j < nu_ref[0])
    def _():
        s = slot_ref[j]

        @pl.when(first_ref[j] == 1)
        def _():
            @pl.when(j == 0)
            def _():
                for cp in weight_copies(te_ref[0], 0):
                    cp.start()

            for cp in weight_copies(te_ref[j], s):
                cp.wait()

            @pl.when(nxt_ref[j] >= 0)
            def _():
                for cp in weight_copies(nxt_ref[j], 1 - s):
                    cp.start()

            wg_sc[...] = wg_f[s].astype(BF16)
            wu_sc[...] = wu_f[s].astype(BF16)
            wd_sc[...] = wd_f[s].astype(BF16)

        t = _rms(xs_ref[...], fg_ref[...]).astype(BF16)
        a = jnp.dot(t, wg_sc[...], preferred_element_type=F32)
        b = jnp.dot(t, wu_sc[...], preferred_element_type=F32)
        hid = (a * jax.nn.sigmoid(a) * b).astype(BF16)
        ys_ref[...] = jnp.dot(hid, wd_sc[...], preferred_element_type=F32)


def _experts(tile_expert, n_used, first, nxt, slot, xs, fg, wg, wu, wd, layer):
    n_rows = xs.shape[0]
    n_tiles = n_rows // TM_EXP
    row_idx = lambda j, te, nu, fi, nx, sl: (jnp.minimum(j, nu[0] - 1), 0)
    hbm = lambda: pl.BlockSpec(memory_space=pl.ANY)
    return pl.pallas_call(
        functools.partial(_expert_body, layer=layer),
        grid_spec=pltpu.PrefetchScalarGridSpec(
            num_scalar_prefetch=5,
            grid=(n_tiles,),
            in_specs=[pl.BlockSpec((TM_EXP, D_MODEL), row_idx),
                      pl.BlockSpec((1, D_MODEL), lambda j, te, nu, fi, nx, sl: (0, 0)),
                      hbm(), hbm(), hbm()],
            out_specs=pl.BlockSpec((TM_EXP, D_MODEL), row_idx),
            scratch_shapes=[pltpu.VMEM((2, D_MODEL, EXPERT_HIDDEN), F32),
                            pltpu.VMEM((2, D_MODEL, EXPERT_HIDDEN), F32),
                            pltpu.VMEM((2, EXPERT_HIDDEN, D_MODEL), F32),
                            pltpu.VMEM((D_MODEL, EXPERT_HIDDEN), BF16),
                            pltpu.VMEM((D_MODEL, EXPERT_HIDDEN), BF16),
                            pltpu.VMEM((EXPERT_HIDDEN, D_MODEL), BF16),
                            pltpu.SemaphoreType.DMA((2, 3))]),
        out_shape=jax.ShapeDtypeStruct((n_rows, D_MODEL), F32),
        compiler_params=_cparams(1),
        name="moe_experts",
    )(tile_expert, n_used, first, nxt, slot, xs, fg, wg, wu, wd)


def _combine_body(pos_ref, posn_ref, x_ref, wts_ref, ys_hbm, o_ref, gbuf, sems, *, n_steps):
    i = pl.program_id(0)
    slot = i % 2

    def issue(p_ref, s):
        def body(r, carry):
            for k in range(2):
                pltpu.make_async_copy(ys_hbm.at[pl.ds(p_ref[0, 0, 2 * r + k], 1)],
                                      gbuf.at[s, k, pl.ds(r, 1)], sems.at[s]).start(priority=k)
            return carry
        lax.fori_loop(0, TC_COMB, body, 0, unroll=DMA_UNROLL)

    @pl.when(i == 0)
    def _():
        issue(pos_ref, 0)

    @pl.when(i + 1 < n_steps)
    def _():
        issue(posn_ref, 1 - slot)

    for k in range(2):
        _bulk_wait(ys_hbm.at[pl.ds(0, TC_COMB)], gbuf.at[slot, k], sems.at[slot])
    w = wts_ref[...]
    o_ref[...] = x_ref[...] + w[:, 0:1] * gbuf[slot, 0] + w[:, 1:2] * gbuf[slot, 1]


def _combine(pos3, x1, wts, ys):
    t_tokens = x1.shape[0]
    n_steps = t_tokens // TC_COMB
    return pl.pallas_call(
        functools.partial(_combine_body, n_steps=n_steps),
        grid=(n_steps,),
        in_specs=[pl.BlockSpec((1, 1, 2 * TC_COMB), lambda i: (i, 0, 0), memory_space=pltpu.SMEM),
                  pl.BlockSpec((1, 1, 2 * TC_COMB), lambda i: (jnp.minimum(i + 1, n_steps - 1), 0, 0),
                               memory_space=pltpu.SMEM),
                  pl.BlockSpec((TC_COMB, D_MODEL), lambda i: (i, 0)),
                  pl.BlockSpec((TC_COMB, LANES), lambda i: (i, 0)),
                  pl.BlockSpec(memory_space=pl.ANY)],
        out_specs=pl.BlockSpec((TC_COMB, D_MODEL), lambda i: (i, 0)),
        out_shape=jax.ShapeDtypeStruct((t_tokens, D_MODEL), F32),
        scratch_shapes=[pltpu.VMEM((2, 2, TC_COMB, D_MODEL), F32),
                        pltpu.SemaphoreType.DMA((2,))],
        compiler_params=_cparams(1),
        name="moe_combine",
    )(pos3, pos3, x1, wts, ys)


def _moe(x1, info, wts, counts, fg, wg, wu, wd, layer):
    t_tokens = x1.shape[0]
    n_tiles = 2 * t_tokens // TM_EXP + N_EXPERTS
    cnt = counts[0, N_GROUPS:N_GROUPS + N_EXPERTS].astype(I32)
    padded = (cnt + TM_EXP - 1) // TM_EXP * TM_EXP
    seg_end = jnp.cumsum(padded)
    seg_start = seg_end - padded
    n_used = seg_end[-1] // TM_EXP
    tile_id = jnp.arange(n_tiles, dtype=I32)
    tile_row = jnp.minimum(tile_id, n_used - 1) * TM_EXP
    tile_expert = jnp.sum((tile_row[:, None] >= seg_end[None, :]).astype(I32), axis=1)
    ids = jnp.arange(N_EXPERTS, dtype=I32)
    eid = info[:, 0:2]
    start_of = jnp.sum(jnp.where(eid[:, :, None] == ids, seg_start, 0), axis=-1)
    pos = start_of + info[:, 2:4]
    xs = _dispatch(pos.reshape(t_tokens // TD_DISP, 1, 2 * TD_DISP), x1, n_tiles * TM_EXP)
    first = jnp.concatenate([jnp.ones((1,), I32), (tile_expert[1:] != tile_expert[:-1]).astype(I32)])
    first = jnp.where(tile_id < n_used, first, 0)
    slot = (jnp.cumsum(first) - 1) % 2
    later = (ids[None, :] > ids[:, None]) & (cnt[None, :] > 0)
    next_of = jnp.min(jnp.where(later, ids[None, :], N_EXPERTS), axis=1)
    next_of = jnp.where(next_of < N_EXPERTS, next_of, -1)
    nxt = jnp.sum(jnp.where(tile_expert[:, None] == ids[None, :], next_of[None, :], 0), axis=1)
    ys = _experts(tile_expert, n_used.reshape(1), first, nxt, slot.astype(I32), xs, fg, wg, wu, wd, layer)
    return _combine(pos.reshape(t_tokens // TC_COMB, 1, 2 * TC_COMB), x1, wts, ys)


def _router_params(w_rg, b_rg, w_re, b_re):
    pad = LANES - N_GROUPS - N_EXPERTS
    wr = jnp.concatenate([w_rg, w_re, jnp.zeros((D_MODEL, pad), F32)], axis=1).astype(BF16)
    br = jnp.concatenate([b_rg, b_re, jnp.zeros((pad,), F32)]).reshape(1, LANES)
    return wr, br


def kernel(x, positions, even_norm, w_in, q_norm, k_norm, conv_w, conv_b, conv_ln_g, conv_ln_b, w_out,
           odd_norm, pool_w, pool_b, pool_scale, ffn_norm, w_router_group, b_router_group,
           w_router_expert, b_router_expert, w_expert_gate, w_expert_up, w_expert_down):
    batch, seq, d = x.shape
    depth = ffn_norm.shape[0]
    assert d == D_MODEL and seq % ATTN_CHUNK == 0 and seq % TM_OUT == 0
    t_tokens = batch * seq
    half = HEAD_DIM // 2
    inv_freq = jnp.float32(ROPE_THETA) ** (-jnp.arange(half, dtype=F32) / half)
    invf2 = jnp.concatenate([inv_freq, inv_freq]).reshape(1, HEAD_DIM)
    pos2 = positions.reshape(t_tokens, 1)
    row = lambda a: a.reshape(1, -1)

    x2 = x.reshape(t_tokens, d)
    for i in range(depth):
        j = i // 2
        wr, br = _router_params(w_router_group[i], b_router_group[i], w_router_expert[i], b_router_expert[i])
        fg = row(ffn_norm[i])
        if i % 2 == 0:
            q, k, v, u = _inproj(x2, pos2, row(even_norm[j]), w_in[j].astype(BF16), row(q_norm[j]),
                                 row(k_norm[j]), invf2, batch, seq)
            attn = _attention(q, k, v, batch, seq)
            conv = _conv(u, conv_w[j], row(conv_b[j]), row(conv_ln_g[j]), row(conv_ln_b[j]), batch, seq)
            x1, info, wts, counts = _outproj_router(attn, conv, x2, w_out[j].astype(BF16), fg, wr, br,
                                                    batch, seq)
        else:
            x1, info, wts, counts = _pool_router(x2, row(odd_norm[j]), pool_w[j].astype(BF16),
                                                 row(pool_b[j]), row(pool_scale[j]), fg, wr, br, batch, seq)
        x2 = _moe(x1, info, wts, counts, fg, w_expert_gate, w_expert_up, w_expert_down, i)
    return x2.reshape(batch, seq, d)
```

```python
import functools

import jax
import jax.numpy as jnp
from jax import lax
from jax.experimental import pallas as pl
from jax.experimental.pallas import tpu as pltpu

F32 = jnp.float32
BF16 = jnp.bfloat16
I32 = jnp.int32

D_MODEL = 2048
HEAD_DIM = 128
N_HEADS = 8
ATTN_WIDTH = N_HEADS * HEAD_DIM
ATTN_BLOCK = 128
DILATIONS = (1, 4, 16)
ATTN_CHUNK = ATTN_BLOCK * max(DILATIONS)
ATTN_UNROLL = 8
MERGE_ROWS = 256
CONV_WIDTH = D_MODEL - ATTN_WIDTH
CONV_KERNEL = 31
CONV_HALO = 32
IN_PROJ_WIDTH = 3 * ATTN_WIDTH + 2 * CONV_WIDTH
POOL_WINDOWS = (2, 4, 8, 16)
POOL_GROUP = D_MODEL // len(POOL_WINDOWS)
POOL_HALO = 16
N_GROUPS = 4
EXPERTS_PER_GROUP = 8
N_EXPERTS = N_GROUPS * EXPERTS_PER_GROUP
EXPERT_HIDDEN = 512
ROPE_THETA = 10000.0
EPS = 1e-6
LANES = 128
NEG = -1e30

TM_PROJ = 256
TN_PROJ = 1024
TM_OUT = 512
ROUTER_ROWS = 256
TC_CONV = 512
RC_CONV = 32
CONV_COLS = 256
SUBLANES = 8
TM_EXP = 256
TD_DISP = 512
TC_COMB = 256
DMA_UNROLL = 8
VMEM_LIMIT = 56 * 1024 * 1024


def _cparams(n_axes):
    return pltpu.CompilerParams(dimension_semantics=("arbitrary",) * n_axes,
                                vmem_limit_bytes=VMEM_LIMIT)


def _rms(x, g):
    return x * lax.rsqrt(jnp.mean(x * x, axis=-1, keepdims=True) + EPS) * g


def _inproj_body(x_ref, pos_ref, g_ref, w_ref, qn_ref, kn_ref, invf_ref, q_ref, k_ref, v_ref, u_ref, hn_sc):
    hn_sc[...] = _rms(x_ref[...], g_ref[...]).astype(BF16)
    ang = pos_ref[...].astype(F32) * invf_ref[...]
    lane = lax.broadcasted_iota(I32, ang.shape, 1)
    cs = jnp.cos(ang)
    sn = jnp.where(lane < HEAD_DIM // 2, -1.0, 1.0) * jnp.sin(ang)

    def section(n):
        return jnp.dot(hn_sc[...], w_ref[:, n * TN_PROJ:(n + 1) * TN_PROJ], preferred_element_type=F32)

    def qk_heads(acc, norm_ref, out_ref, scale):
        for h in range(N_HEADS):
            y = _rms(acc[:, h * HEAD_DIM:(h + 1) * HEAD_DIM], norm_ref[...])
            y = y * cs + pltpu.roll(y, HEAD_DIM // 2, 1) * sn
            out_ref[0, h] = (y * scale).astype(BF16)

    qk_heads(section(0), qn_ref, q_ref, HEAD_DIM ** -0.5)
    qk_heads(section(1), kn_ref, k_ref, 1.0)
    acc = section(2)
    for h in range(N_HEADS):
        v_ref[0, h] = acc[:, h * HEAD_DIM:(h + 1) * HEAD_DIM].astype(BF16)
    u_ref[:, :TN_PROJ] = section(3)
    u_ref[:, TN_PROJ:] = section(4)


def _inproj(x2, pos2, g, w_bf, qn, kn, invf2, batch, seq):
    t_tokens = batch * seq
    nsb = seq // TM_PROJ
    head_spec = pl.BlockSpec((1, N_HEADS, TM_PROJ, HEAD_DIM), lambda i: (i // nsb, 0, i % nsb, 0))
    head_shape = jax.ShapeDtypeStruct((batch, N_HEADS, seq, HEAD_DIM), BF16)
    return pl.pallas_call(
        _inproj_body,
        grid=(t_tokens // TM_PROJ,),
        in_specs=[
            pl.BlockSpec((TM_PROJ, D_MODEL), lambda i: (i, 0)),
            pl.BlockSpec((TM_PROJ, 1), lambda i: (i, 0)),
            pl.BlockSpec((1, D_MODEL), lambda i: (0, 0)),
            pl.BlockSpec((D_MODEL, IN_PROJ_WIDTH), lambda i: (0, 0), pipeline_mode=pl.Buffered(1)),
            pl.BlockSpec((1, HEAD_DIM), lambda i: (0, 0)),
            pl.BlockSpec((1, HEAD_DIM), lambda i: (0, 0)),
            pl.BlockSpec((1, HEAD_DIM), lambda i: (0, 0)),
        ],
        out_specs=[head_spec, head_spec, head_spec,
                   pl.BlockSpec((TM_PROJ, 2 * CONV_WIDTH), lambda i: (i, 0))],
        out_shape=[head_shape, head_shape, head_shape,
                   jax.ShapeDtypeStruct((t_tokens, 2 * CONV_WIDTH), F32)],
        scratch_shapes=[pltpu.VMEM((TM_PROJ, D_MODEL), BF16)],
        compiler_params=_cparams(1),
        name="inproj",
    )(x2, pos2, g, w_bf, qn, kn, invf2)


def _attn_body(q_ref, ko_ref, kp_ref, vo_ref, vp_ref, o_ref, qf, kf, vf, m_sc, l_sc, acc_sc):
    c = pl.program_id(2)
    ch = ATTN_CHUNK
    qf[...] = q_ref[0, 0].astype(F32)
    kf[:ch] = kp_ref[0, 0].astype(F32)
    kf[ch:] = ko_ref[0, 0].astype(F32)
    vf[:ch] = vp_ref[0, 0].astype(F32)
    vf[ch:] = vo_ref[0, 0].astype(F32)

    qi = lax.broadcasted_iota(I32, (ATTN_BLOCK, 2 * ATTN_BLOCK), 0)
    kj = lax.broadcasted_iota(I32, (ATTN_BLOCK, 2 * ATTN_BLOCK), 1)
    band = (kj >= qi) & (kj <= qi + ATTN_BLOCK)
    bias_all = jnp.where(band, 0.0, NEG)
    bias_own = jnp.where(band & (kj >= ATTN_BLOCK), 0.0, NEG)

    def rows(start, n, d):
        return pl.ds(start, n) if d == 1 else pl.ds(start, n, stride=d)

    def block(t, d, p_idx):
        unit = ATTN_BLOCK * d
        qs = (t // d) * unit + (t % d)
        ks = ch + qs - unit
        q = qf[rows(qs, ATTN_BLOCK, d), :].astype(BF16)
        kc = kf[rows(ks, 2 * ATTN_BLOCK, d), :].astype(BF16)
        vc = vf[rows(ks, 2 * ATTN_BLOCK, d), :].astype(BF16)
        s = lax.dot_general(q, kc, (((1,), (1,)), ((), ())), preferred_element_type=F32)
        has_prev = jnp.logical_or(c > 0, ks >= ch)
        s = s + jnp.where(has_prev, bias_all, bias_own)
        m_b = jnp.max(s, axis=-1, keepdims=True)
        p = jnp.exp(s - m_b)
        r = rows(qs, ATTN_BLOCK, d)
        full = (ATTN_BLOCK, HEAD_DIM)
        m_sc[p_idx, r, :] = jnp.broadcast_to(m_b, full)
        l_sc[p_idx, r, :] = jnp.broadcast_to(jnp.sum(p, axis=-1, keepdims=True), full)
        acc_sc[p_idx, r, :] = jnp.dot(p.astype(BF16), vc, preferred_element_type=F32)

    for p_idx, d in enumerate(DILATIONS):
        def step(t, carry, d=d, p_idx=p_idx):
            block(t, d, p_idx)
            return carry
        lax.fori_loop(0, ch // ATTN_BLOCK, step, 0, unroll=ATTN_UNROLL)

    def merge(i, carry):
        r = pl.ds(pl.multiple_of(i * MERGE_ROWS, MERGE_ROWS), MERGE_ROWS)
        ms = [m_sc[p_idx, r, :] for p_idx in range(len(DILATIONS))]
        m = functools.reduce(jnp.maximum, ms)
        num = jnp.zeros((MERGE_ROWS, HEAD_DIM), F32)
        den = jnp.zeros((MERGE_ROWS, HEAD_DIM), F32)
        for p_idx in range(len(DILATIONS)):
            a = jnp.exp(ms[p_idx] - m)
            num = num + a * acc_sc[p_idx, r, :]
            den = den + a * l_sc[p_idx, r, :]
        o_ref[0, 0, r, :] = (num / den).astype(BF16)
        return carry

    lax.fori_loop(0, ch // MERGE_ROWS, merge, 0)


def _attention(q, k, v, batch, seq):
    ch = ATTN_CHUNK
    own = pl.BlockSpec((1, 1, ch, HEAD_DIM), lambda b, h, c: (b, h, c, 0))
    prev = pl.BlockSpec((1, 1, ch, HEAD_DIM), lambda b, h, c: (b, h, jnp.maximum(c - 1, 0), 0))
    return pl.pallas_call(
        _attn_body,
        grid=(batch, N_HEADS, seq // ch),
        in_specs=[own, own, prev, own, prev],
        out_specs=own,
        out_shape=jax.ShapeDtypeStruct((batch, N_HEADS, seq, HEAD_DIM), BF16),
        scratch_shapes=[pltpu.VMEM((ch, HEAD_DIM), F32),
                        pltpu.VMEM((2 * ch, HEAD_DIM), F32),
                        pltpu.VMEM((2 * ch, HEAD_DIM), F32),
                        pltpu.VMEM((len(DILATIONS), ch, HEAD_DIM), F32),
                        pltpu.VMEM((len(DILATIONS), ch, HEAD_DIM), F32),
                        pltpu.VMEM((len(DILATIONS), ch, HEAD_DIM), F32)],
        compiler_params=_cparams(3),
        name="dilated_attn",
    )(q, k, k, v, v)


def _conv_body(um_ref, uh_ref, w_ref, b_ref, g_ref, bb_ref, o_ref, y_sc, z_sc):
    i = pl.program_id(1)
    cw = CONV_WIDTH
    uh = uh_ref[...]
    yh = uh[:, :cw] * jax.nn.sigmoid(uh[:, cw:])
    y_sc[:CONV_HALO] = jnp.where(i > 0, yh, 0.0)

    def glu(rc, carry):
        r0 = pl.multiple_of(rc * RC_CONV, RC_CONV)
        um = um_ref[pl.ds(r0, RC_CONV), :]
        y_sc[pl.ds(CONV_HALO + r0, RC_CONV), :] = um[:, :cw] * jax.nn.sigmoid(um[:, cw:])
        return carry

    lax.fori_loop(0, TC_CONV // RC_CONV, glu, 0)

    win_rows = RC_CONV + CONV_HALO
    first_off = CONV_HALO - (CONV_KERNEL - 1)

    def chunk(rc, carry):
        r0 = pl.multiple_of(rc * RC_CONV, RC_CONV)
        for cg in range(cw // CONV_COLS):
            cols = slice(cg * CONV_COLS, (cg + 1) * CONV_COLS)
            win = y_sc[pl.ds(r0, win_rows), cols]
            acc = jnp.zeros((RC_CONV, CONV_COLS), F32)
            for phase in range(SUBLANES):
                shifted = win if phase == 0 else pltpu.roll(win, win_rows - phase, 0)
                for k in range(CONV_KERNEL):
                    off = first_off + k
                    if off % SUBLANES == phase:
                        base = off - phase
                        acc = acc + shifted[base:base + RC_CONV] * w_ref[k:k + 1, cols]
            z_sc[:, cols] = acc
        yf = z_sc[...] + b_ref[...]
        mu = jnp.mean(yf, axis=-1, keepdims=True)
        yc = yf - mu
        var = jnp.mean(yc * yc, axis=-1, keepdims=True)
        yn = yc * lax.rsqrt(var + EPS) * g_ref[...] + bb_ref[...]
        o_ref[pl.ds(r0, RC_CONV), :] = (yn * jax.nn.sigmoid(yn)).astype(BF16)
        return carry

    lax.fori_loop(0, TC_CONV // RC_CONV, chunk, 0)


def _conv(u, conv_w, conv_b, ln_g, ln_b, batch, seq):
    nsb = seq // TC_CONV
    hpt = TC_CONV // CONV_HALO
    row = lambda: pl.BlockSpec((1, CONV_WIDTH), lambda b, i: (0, 0))
    return pl.pallas_call(
        _conv_body,
        grid=(batch, nsb),
        in_specs=[
            pl.BlockSpec((TC_CONV, 2 * CONV_WIDTH), lambda b, i: (b * nsb + i, 0)),
            pl.BlockSpec((CONV_HALO, 2 * CONV_WIDTH),
                         lambda b, i: (jnp.maximum((b * nsb + i) * hpt - 1, 0), 0)),
            pl.BlockSpec((CONV_KERNEL, CONV_WIDTH), lambda b, i: (0, 0)),
            row(), row(), row(),
        ],
        out_specs=pl.BlockSpec((TC_CONV, CONV_WIDTH), lambda b, i: (b * nsb + i, 0)),
        out_shape=jax.ShapeDtypeStruct((batch * seq, CONV_WIDTH), BF16),
        scratch_shapes=[pltpu.VMEM((CONV_HALO + TC_CONV, CONV_WIDTH), F32),
                        pltpu.VMEM((RC_CONV, CONV_WIDTH), F32)],
        compiler_params=_cparams(2),
        name="conformer_conv",
    )(u, u, conv_w, conv_b, ln_g, ln_b)


def _router(x1, rows, fg_ref, wr_ref, br_ref, info_ref, wts_ref, cnt_ref, carry_sc):
    tm = x1.shape[0]
    t = _rms(x1, fg_ref[...]).astype(BF16)
    logits = jnp.dot(t, wr_ref[...], preferred_element_type=F32) + br_ref[...]
    lane = lax.broadcasted_iota(I32, (tm, LANES), 1)
    lanef = lane.astype(F32)
    ninf = -jnp.inf

    def first_argmax(vals):
        top = jnp.max(vals, axis=-1, keepdims=True)
        idx = jnp.min(jnp.where(vals == top, lanef, float(LANES)), axis=-1, keepdims=True)
        return top, idx

    gl = jnp.where(lane < N_GROUPS, logits, ninf)
    gmax, gidx = first_argmax(gl)
    p_sel = 1.0 / jnp.sum(jnp.exp(gl - gmax), axis=-1, keepdims=True)
    lo = N_GROUPS + EXPERTS_PER_GROUP * gidx
    el = jnp.where((lanef >= lo) & (lanef < lo + EXPERTS_PER_GROUP), logits, ninf)
    v1, i1 = first_argmax(el)
    el2 = jnp.where(lanef == i1, ninf, el)
    v2, i2 = first_argmax(el2)
    e21 = jnp.exp(v2 - v1)
    w1 = p_sel / (1.0 + e21)
    w2 = p_sel * e21 / (1.0 + e21)

    hit1 = lanef == i1
    hit2 = lanef == i2
    mh = jnp.where(hit1 | hit2, 1.0, 0.0)
    ri = lax.broadcasted_iota(I32, (tm, tm), 0)
    ci = lax.broadcasted_iota(I32, (tm, tm), 1)
    tri = jnp.where(ci < ri, 1.0, 0.0).astype(BF16)
    before = jnp.dot(tri, mh.astype(BF16), preferred_element_type=F32) + carry_sc[...]
    rank1 = jnp.sum(jnp.where(hit1, before, 0.0), axis=-1, keepdims=True)
    rank2 = jnp.sum(jnp.where(hit2, before, 0.0), axis=-1, keepdims=True)
    carry_sc[...] = carry_sc[...] + jnp.sum(mh, axis=0, keepdims=True)
    cnt_ref[...] = carry_sc[...]

    info = jnp.where(lane == 0, i1 - N_GROUPS,
                     jnp.where(lane == 1, i2 - N_GROUPS,
                               jnp.where(lane == 2, rank1, jnp.where(lane == 3, rank2, 0.0))))
    info_ref[rows, :] = info.astype(I32)
    wts_ref[rows, :] = jnp.where(lane == 0, w1, jnp.where(lane == 1, w2, 0.0))


def _router_specs(tm):
    in_specs = [pl.BlockSpec((1, D_MODEL), lambda i: (0, 0)),
                pl.BlockSpec((D_MODEL, LANES), lambda i: (0, 0)),
                pl.BlockSpec((1, LANES), lambda i: (0, 0))]
    out_specs = [pl.BlockSpec((tm, LANES), lambda i: (i, 0)),
                 pl.BlockSpec((tm, LANES), lambda i: (i, 0)),
                 pl.BlockSpec((1, LANES), lambda i: (0, 0))]
    return in_specs, out_specs


def _router_shapes(t_tokens):
    return [jax.ShapeDtypeStruct((t_tokens, LANES), I32),
            jax.ShapeDtypeStruct((t_tokens, LANES), F32),
            jax.ShapeDtypeStruct((1, LANES), F32)]


def _zero_carry_on_first_step(carry_sc):
    @pl.when(pl.program_id(0) == 0)
    def _():
        carry_sc[...] = jnp.zeros_like(carry_sc)


def _outproj_body(attn_ref, conv_ref, x_ref, wo_ref, fg_ref, wr_ref, br_ref,
                  x1_ref, info_ref, wts_ref, cnt_ref, carry_sc):
    _zero_carry_on_first_step(carry_sc)
    for half in range(TM_OUT // ROUTER_ROWS):
        rows = pl.ds(half * ROUTER_ROWS, ROUTER_ROWS)
        cat = jnp.concatenate([attn_ref[0, h, rows, :] for h in range(N_HEADS)] + [conv_ref[rows, :]], axis=-1)
        x1 = x_ref[rows, :] + jnp.dot(cat, wo_ref[...], preferred_element_type=F32)
        x1_ref[rows, :] = x1
        _router(x1, rows, fg_ref, wr_ref, br_ref, info_ref, wts_ref, cnt_ref, carry_sc)


def _outproj_router(attn, conv, x2, wo_bf, fg, wr_bf, br, batch, seq):
    t_tokens = batch * seq
    tm = TM_OUT
    nsb = seq // tm
    r_in, r_out = _router_specs(tm)
    return pl.pallas_call(
        _outproj_body,
        grid=(t_tokens // tm,),
        in_specs=[pl.BlockSpec((1, N_HEADS, tm, HEAD_DIM), lambda i: (i // nsb, 0, i % nsb, 0)),
                  pl.BlockSpec((tm, CONV_WIDTH), lambda i: (i, 0)),
                  pl.BlockSpec((tm, D_MODEL), lambda i: (i, 0)),
                  pl.BlockSpec((D_MODEL, D_MODEL), lambda i: (0, 0))] + r_in,
        out_specs=[pl.BlockSpec((tm, D_MODEL), lambda i: (i, 0))] + r_out,
        out_shape=[jax.ShapeDtypeStruct((t_tokens, D_MODEL), F32)] + _router_shapes(t_tokens),
        scratch_shapes=[pltpu.VMEM((1, LANES), F32)],
        compiler_params=_cparams(1),
        name="outproj_router",
    )(attn, conv, x2, wo_bf, fg, wr_bf, br)


def _pool_body(x_ref, xh_ref, og_ref, pw_ref, pb_ref, ps_ref, fg_ref, wr_ref, br_ref,
               x1_ref, info_ref, wts_ref, cnt_ref, carry_sc, h_sc, *, tiles_per_seq):
    tm = x_ref.shape[0]
    ts = pl.program_id(0) % tiles_per_seq
    _zero_carry_on_first_step(carry_sc)
    h_sc[:POOL_HALO] = jnp.where(ts > 0, _rms(xh_ref[...], og_ref[...]), 0.0)
    h_sc[POOL_HALO:] = _rms(x_ref[...], og_ref[...])
    for half in range(tm // ROUTER_ROWS):
        r0 = half * ROUTER_ROWS
        rows = pl.ds(r0, ROUTER_ROWS)
        tpos = ts * tm + r0 + lax.broadcasted_iota(I32, (ROUTER_ROWS, 1), 0) + 1
        ys = []
        for g, w in enumerate(POOL_WINDOWS):
            cols = slice(g * POOL_GROUP, (g + 1) * POOL_GROUP)
            ext = h_sc[r0:r0 + POOL_HALO + ROUTER_ROWS, cols]
            tot = ext
            span = 1
            while span < w:
                tot = tot + pltpu.roll(tot, span, 0)
                span *= 2
            cur = ext[POOL_HALO:]
            cnt = jnp.minimum(tpos, w).astype(F32)
            mixed = tot[POOL_HALO:] / cnt - cur
            ys.append(jnp.dot(mixed.astype(BF16), pw_ref[g], preferred_element_type=F32))
        y = jnp.concatenate(ys, axis=-1)
        x1 = x_ref[rows, :] + (y + pb_ref[...]) * ps_ref[...]
        x1_ref[rows, :] = x1
        _router(x1, rows, fg_ref, wr_ref, br_ref, info_ref, wts_ref, cnt_ref, carry_sc)


def _pool_router(x2, og, pw_bf, pb, ps, fg, wr_bf, br, batch, seq):
    t_tokens = batch * seq
    tm = TM_OUT
    hpt = tm // POOL_HALO
    r_in, r_out = _router_specs(tm)
    vec = lambda: pl.BlockSpec((1, D_MODEL), lambda i: (0, 0))
    return pl.pallas_call(
        functools.partial(_pool_body, tiles_per_seq=seq // tm),
        grid=(t_tokens // tm,),
        in_specs=[pl.BlockSpec((tm, D_MODEL), lambda i: (i, 0)),
                  pl.BlockSpec((POOL_HALO, D_MODEL), lambda i: (jnp.maximum(i * hpt - 1, 0), 0)),
                  vec(),
                  pl.BlockSpec((len(POOL_WINDOWS), POOL_GROUP, POOL_GROUP), lambda i: (0, 0, 0)),
                  vec(), vec()] + r_in,
        out_specs=[pl.BlockSpec((tm, D_MODEL), lambda i: (i, 0))] + r_out,
        out_shape=[jax.ShapeDtypeStruct((t_tokens, D_MODEL), F32)] + _router_shapes(t_tokens),
        scratch_shapes=[pltpu.VMEM((1, LANES), F32),
                        pltpu.VMEM((POOL_HALO + tm, D_MODEL), F32)],
        compiler_params=_cparams(1),
        name="pool_router",
    )(x2, x2, og, pw_bf, pb, ps, fg, wr_bf, br)


def _bulk_wait(src, dst, sem):
    pltpu.make_async_copy(src, dst, sem).wait()


def _dispatch_body(pos_ref, x_ref, xs_hbm, sem):
    def issue(r, carry):
        for k in range(2):
            pltpu.make_async_copy(x_ref.at[pl.ds(r, 1)],
                                  xs_hbm.at[pl.ds(pos_ref[0, 0, k * TD_DISP + r], 1)], sem).start()
        return carry

    lax.fori_loop(0, TD_DISP, issue, 0, unroll=DMA_UNROLL)
    for k in range(2):
        _bulk_wait(x_ref, xs_hbm.at[pl.ds(0, TD_DISP)], sem)


def _dispatch(pos3, x1, n_rows):
    t_tokens = x1.shape[0]
    return pl.pallas_call(
        _dispatch_body,
        grid=(t_tokens // TD_DISP,),
        in_specs=[pl.BlockSpec((1, 1, 2 * TD_DISP), lambda i: (i, 0, 0), memory_space=pltpu.SMEM),
                  pl.BlockSpec((TD_DISP, D_MODEL), lambda i: (i, 0))],
        out_specs=pl.BlockSpec(memory_space=pl.ANY),
        out_shape=jax.ShapeDtypeStruct((n_rows, D_MODEL), F32),
        scratch_shapes=[pltpu.SemaphoreType.DMA(())],
        compiler_params=_cparams(1),
        name="moe_dispatch",
    )(pos3, x1)


def _expert_body(te_ref, nu_ref, first_ref, nxt_ref, slot_ref, xs_ref, fg_ref, wg_hbm, wu_hbm, wd_hbm, ys_ref,
                 wg_f, wu_f, wd_f, wg_sc, wu_sc, wd_sc, sems, *, layer):
    j = pl.program_id(0)

    def weight_copies(e, s):
        return (pltpu.make_async_copy(wg_hbm.at[layer, e], wg_f.at[s], sems.at[s, 0]),
                pltpu.make_async_copy(wu_hbm.at[layer, e], wu_f.at[s], sems.at[s, 1]),
                pltpu.make_async_copy(wd_hbm.at[layer, e], wd_f.at[s], sems.at[s, 2]))

    @pl.when(j < nu_ref[0])
    def _():
        s = slot_ref[j]

        @pl.when(first_ref[j] == 1)
        def _():
            @pl.when(j == 0)
            def _():
                for cp in weight_copies(te_ref[0], 0):
                    cp.start()

            for cp in weight_copies(te_ref[j], s):
                cp.wait()

            @pl.when(nxt_ref[j] >= 0)
            def _():
                for cp in weight_copies(nxt_ref[j], 1 - s):
                    cp.start()

            wg_sc[...] = wg_f[s].astype(BF16)
            wu_sc[...] = wu_f[s].astype(BF16)
            wd_sc[...] = wd_f[s].astype(BF16)

        t = _rms(xs_ref[...], fg_ref[...]).astype(BF16)
        a = jnp.dot(t, wg_sc[...], preferred_element_type=F32)
        b = jnp.dot(t, wu_sc[...], preferred_element_type=F32)
        hid = (a * jax.nn.sigmoid(a) * b).astype(BF16)
        ys_ref[...] = jnp.dot(hid, wd_sc[...], preferred_element_type=F32)


def _experts(tile_expert, n_used, first, nxt, slot, xs, fg, wg, wu, wd, layer):
    n_rows = xs.shape[0]
    n_tiles = n_rows // TM_EXP
    row_idx = lambda j, te, nu, fi, nx, sl: (jnp.minimum(j, nu[0] - 1), 0)
    hbm = lambda: pl.BlockSpec(memory_space=pl.ANY)
    return pl.pallas_call(
        functools.partial(_expert_body, layer=layer),
        grid_spec=pltpu.PrefetchScalarGridSpec(
            num_scalar_prefetch=5,
            grid=(n_tiles,),
            in_specs=[pl.BlockSpec((TM_EXP, D_MODEL), row_idx),
                      pl.BlockSpec((1, D_MODEL), lambda j, te, nu, fi, nx, sl: (0, 0)),
                      hbm(), hbm(), hbm()],
            out_specs=pl.BlockSpec((TM_EXP, D_MODEL), row_idx),
            scratch_shapes=[pltpu.VMEM((2, D_MODEL, EXPERT_HIDDEN), F32),
                            pltpu.VMEM((2, D_MODEL, EXPERT_HIDDEN), F32),
                            pltpu.VMEM((2, EXPERT_HIDDEN, D_MODEL), F32),
                            pltpu.VMEM((D_MODEL, EXPERT_HIDDEN), BF16),
                            pltpu.VMEM((D_MODEL, EXPERT_HIDDEN), BF16),
                            pltpu.VMEM((EXPERT_HIDDEN, D_MODEL), BF16),
                            pltpu.SemaphoreType.DMA((2, 3))]),
        out_shape=jax.ShapeDtypeStruct((n_rows, D_MODEL), F32),
        compiler_params=_cparams(1),
        name="moe_experts",
    )(tile_expert, n_used, first, nxt, slot, xs, fg, wg, wu, wd)


def _combine_body(pos_ref, posn_ref, x_ref, wts_ref, ys_hbm, o_ref, gbuf, sems, *, n_steps):
    i = pl.program_id(0)
    slot = i % 2

    def issue(p_ref, s):
        def body(r, carry):
            for k in range(2):
                pltpu.make_async_copy(ys_hbm.at[pl.ds(p_ref[0, 0, k * TC_COMB + r], 1)],
                                      gbuf.at[s, k, pl.ds(r, 1)], sems.at[s]).start()
            return carry
        lax.fori_loop(0, TC_COMB, body, 0, unroll=DMA_UNROLL)

    @pl.when(i == 0)
    def _():
        issue(pos_ref, 0)

    @pl.when(i + 1 < n_steps)
    def _():
        issue(posn_ref, 1 - slot)

    for k in range(2):
        _bulk_wait(ys_hbm.at[pl.ds(0, TC_COMB)], gbuf.at[slot, k], sems.at[slot])
    w = wts_ref[...]
    o_ref[...] = x_ref[...] + w[:, 0:1] * gbuf[slot, 0] + w[:, 1:2] * gbuf[slot, 1]


def _combine(pos3, x1, wts, ys):
    t_tokens = x1.shape[0]
    n_steps = t_tokens // TC_COMB
    return pl.pallas_call(
        functools.partial(_combine_body, n_steps=n_steps),
        grid=(n_steps,),
        in_specs=[pl.BlockSpec((1, 1, 2 * TC_COMB), lambda i: (i, 0, 0), memory_space=pltpu.SMEM),
                  pl.BlockSpec((1, 1, 2 * TC_COMB), lambda i: (jnp.minimum(i + 1, n_steps - 1), 0, 0),
                               memory_space=pltpu.SMEM),
                  pl.BlockSpec((TC_COMB, D_MODEL), lambda i: (i, 0)),
                  pl.BlockSpec((TC_COMB, LANES), lambda i: (i, 0)),
                  pl.BlockSpec(memory_space=pl.ANY)],
        out_specs=pl.BlockSpec((TC_COMB, D_MODEL), lambda i: (i, 0)),
        out_shape=jax.ShapeDtypeStruct((t_tokens, D_MODEL), F32),
        scratch_shapes=[pltpu.VMEM((2, 2, TC_COMB, D_MODEL), F32),
                        pltpu.SemaphoreType.DMA((2,))],
        compiler_params=_cparams(1),
        name="moe_combine",
    )(pos3, pos3, x1, wts, ys)


def _moe(x1, info, wts, counts, fg, wg, wu, wd, layer):
    t_tokens = x1.shape[0]
    n_tiles = 2 * t_tokens // TM_EXP + N_EXPERTS
    cnt = counts[0, N_GROUPS:N_GROUPS + N_EXPERTS].astype(I32)
    padded = (cnt + TM_EXP - 1) // TM_EXP * TM_EXP
    seg_end = jnp.cumsum(padded)
    seg_start = seg_end - padded
    n_used = seg_end[-1] // TM_EXP
    tile_id = jnp.arange(n_tiles, dtype=I32)
    tile_row = jnp.minimum(tile_id, n_used - 1) * TM_EXP
    tile_expert = jnp.sum((tile_row[:, None] >= seg_end[None, :]).astype(I32), axis=1)
    ids = jnp.arange(N_EXPERTS, dtype=I32)

    def slot_rows(k):
        start = jnp.sum(jnp.where(info[:, k][:, None] == ids, seg_start, 0), axis=-1)
        return start + info[:, 2 + k]

    def per_tile(n):
        return jnp.concatenate([p.reshape(t_tokens // n, 1, n) for p in pos], axis=2)

    pos = [slot_rows(0), slot_rows(1)]
    xs = _dispatch(per_tile(TD_DISP), x1, n_tiles * TM_EXP)
    first = jnp.concatenate([jnp.ones((1,), I32), (tile_expert[1:] != tile_expert[:-1]).astype(I32)])
    first = jnp.where(tile_id < n_used, first, 0)
    slot = (jnp.cumsum(first) - 1) % 2
    later = (ids[None, :] > ids[:, None]) & (cnt[None, :] > 0)
    next_of = jnp.min(jnp.where(later, ids[None, :], N_EXPERTS), axis=1)
    next_of = jnp.where(next_of < N_EXPERTS, next_of, -1)
    nxt = jnp.sum(jnp.where(tile_expert[:, None] == ids[None, :], next_of[None, :], 0), axis=1)
    ys = _experts(tile_expert, n_used.reshape(1), first, nxt, slot.astype(I32), xs, fg, wg, wu, wd, layer)
    return _combine(per_tile(TC_COMB), x1, wts, ys)


def _router_params(w_rg, b_rg, w_re, b_re):
    pad = LANES - N_GROUPS - N_EXPERTS
    wr = jnp.concatenate([w_rg, w_re, jnp.zeros((D_MODEL, pad), F32)], axis=1).astype(BF16)
    br = jnp.concatenate([b_rg, b_re, jnp.zeros((pad,), F32)]).reshape(1, LANES)
    return wr, br


def kernel(x, positions, even_norm, w_in, q_norm, k_norm, conv_w, conv_b, conv_ln_g, conv_ln_b, w_out,
           odd_norm, pool_w, pool_b, pool_scale, ffn_norm, w_router_group, b_router_group,
           w_router_expert, b_router_expert, w_expert_gate, w_expert_up, w_expert_down):
    batch, seq, d = x.shape
    depth = ffn_norm.shape[0]
    assert d == D_MODEL and seq % ATTN_CHUNK == 0 and seq % TM_OUT == 0
    t_tokens = batch * seq
    half = HEAD_DIM // 2
    inv_freq = jnp.float32(ROPE_THETA) ** (-jnp.arange(half, dtype=F32) / half)
    invf2 = jnp.concatenate([inv_freq, inv_freq]).reshape(1, HEAD_DIM)
    pos2 = positions.reshape(t_tokens, 1)
    row = lambda a: a.reshape(1, -1)

    x2 = x.reshape(t_tokens, d)
    for i in range(depth):
        j = i // 2
        wr, br = _router_params(w_router_group[i], b_router_group[i], w_router_expert[i], b_router_expert[i])
        fg = row(ffn_norm[i])
        if i % 2 == 0:
            q, k, v, u = _inproj(x2, pos2, row(even_norm[j]), w_in[j].astype(BF16), row(q_norm[j]),
                                 row(k_norm[j]), invf2, batch, seq)
            attn = _attention(q, k, v, batch, seq)
            conv = _conv(u, conv_w[j], row(conv_b[j]), row(conv_ln_g[j]), row(conv_ln_b[j]), batch, seq)
            x1, info, wts, counts = _outproj_router(attn, conv, x2, w_out[j].astype(BF16), fg, wr, br,
                                                    batch, seq)
        else:
            x1, info, wts, counts = _pool_router(x2, row(odd_norm[j]), pool_w[j].astype(BF16),
                                                 row(pool_b[j]), row(pool_scale[j]), fg, wr, br, batch, seq)
        x2 = _moe(x1, info, wts, counts, fg, w_expert_gate, w_expert_up, w_expert_down, i)
    return x2.reshape(batch, seq, d)
```

```python
import functools

import jax
import jax.numpy as jnp
from jax import lax
from jax.experimental import pallas as pl
from jax.experimental.pallas import tpu as pltpu

F32 = jnp.float32
BF16 = jnp.bfloat16
I32 = jnp.int32

D_MODEL = 2048
HEAD_DIM = 128
N_HEADS = 8
ATTN_WIDTH = N_HEADS * HEAD_DIM
ATTN_BLOCK = 128
DILATIONS = (1, 4, 16)
ATTN_CHUNK = ATTN_BLOCK * max(DILATIONS)
ATTN_UNROLL = 8
MERGE_ROWS = 256
CONV_WIDTH = D_MODEL - ATTN_WIDTH
CONV_KERNEL = 31
CONV_HALO = 32
IN_PROJ_WIDTH = 3 * ATTN_WIDTH + 2 * CONV_WIDTH
POOL_WINDOWS = (2, 4, 8, 16)
POOL_GROUP = D_MODEL // len(POOL_WINDOWS)
POOL_HALO = 16
N_GROUPS = 4
EXPERTS_PER_GROUP = 8
N_EXPERTS = N_GROUPS * EXPERTS_PER_GROUP
EXPERT_HIDDEN = 512
ROPE_THETA = 10000.0
EPS = 1e-6
LANES = 128
NEG = -1e30

TM_PROJ = 256
TN_PROJ = 1024
TM_OUT = 512
ROUTER_ROWS = 256
TC_CONV = 512
RC_CONV = 32
CONV_COLS = 256
SUBLANES = 8
TM_EXP = 256
TC_COMB = 256
DMA_UNROLL = 8
VMEM_LIMIT = 56 * 1024 * 1024


def _cparams(n_axes):
    return pltpu.CompilerParams(dimension_semantics=("arbitrary",) * n_axes,
                                vmem_limit_bytes=VMEM_LIMIT)


def _rms(x, g):
    return x * lax.rsqrt(jnp.mean(x * x, axis=-1, keepdims=True) + EPS) * g


def _inproj_body(x_ref, pos_ref, g_ref, w_ref, qn_ref, kn_ref, invf_ref, q_ref, k_ref, v_ref, u_ref, hn_sc):
    hn_sc[...] = _rms(x_ref[...], g_ref[...]).astype(BF16)
    ang = pos_ref[...].astype(F32) * invf_ref[...]
    lane = lax.broadcasted_iota(I32, ang.shape, 1)
    cs = jnp.cos(ang)
    sn = jnp.where(lane < HEAD_DIM // 2, -1.0, 1.0) * jnp.sin(ang)

    def section(n):
        return jnp.dot(hn_sc[...], w_ref[:, n * TN_PROJ:(n + 1) * TN_PROJ], preferred_element_type=F32)

    def qk_heads(acc, norm_ref, out_ref, scale):
        for h in range(N_HEADS):
            y = _rms(acc[:, h * HEAD_DIM:(h + 1) * HEAD_DIM], norm_ref[...])
            y = y * cs + pltpu.roll(y, HEAD_DIM // 2, 1) * sn
            out_ref[0, h] = (y * scale).astype(BF16)

    qk_heads(section(0), qn_ref, q_ref, HEAD_DIM ** -0.5)
    qk_heads(section(1), kn_ref, k_ref, 1.0)
    acc = section(2)
    for h in range(N_HEADS):
        v_ref[0, h] = acc[:, h * HEAD_DIM:(h + 1) * HEAD_DIM].astype(BF16)
    u_ref[:, :TN_PROJ] = section(3)
    u_ref[:, TN_PROJ:] = section(4)


def _inproj(x2, pos2, g, w_bf, qn, kn, invf2, batch, seq):
    t_tokens = batch * seq
    nsb = seq // TM_PROJ
    head_spec = pl.BlockSpec((1, N_HEADS, TM_PROJ, HEAD_DIM), lambda i: (i // nsb, 0, i % nsb, 0))
    head_shape = jax.ShapeDtypeStruct((batch, N_HEADS, seq, HEAD_DIM), BF16)
    return pl.pallas_call(
        _inproj_body,
        grid=(t_tokens // TM_PROJ,),
        in_specs=[
            pl.BlockSpec((TM_PROJ, D_MODEL), lambda i: (i, 0)),
            pl.BlockSpec((TM_PROJ, 1), lambda i: (i, 0)),
            pl.BlockSpec((1, D_MODEL), lambda i: (0, 0)),
            pl.BlockSpec((D_MODEL, IN_PROJ_WIDTH), lambda i: (0, 0), pipeline_mode=pl.Buffered(1)),
            pl.BlockSpec((1, HEAD_DIM), lambda i: (0, 0)),
            pl.BlockSpec((1, HEAD_DIM), lambda i: (0, 0)),
            pl.BlockSpec((1, HEAD_DIM), lambda i: (0, 0)),
        ],
        out_specs=[head_spec, head_spec, head_spec,
                   pl.BlockSpec((TM_PROJ, 2 * CONV_WIDTH), lambda i: (i, 0))],
        out_shape=[head_shape, head_shape, head_shape,
                   jax.ShapeDtypeStruct((t_tokens, 2 * CONV_WIDTH), F32)],
        scratch_shapes=[pltpu.VMEM((TM_PROJ, D_MODEL), BF16)],
        compiler_params=_cparams(1),
        name="inproj",
    )(x2, pos2, g, w_bf, qn, kn, invf2)


def _attn_body(q_ref, ko_ref, kp_ref, vo_ref, vp_ref, o_ref, qf, kf, vf, m_sc, l_sc, acc_sc):
    c = pl.program_id(2)
    ch = ATTN_CHUNK
    qf[...] = q_ref[0, 0].astype(F32)
    kf[:ch] = kp_ref[0, 0].astype(F32)
    kf[ch:] = ko_ref[0, 0].astype(F32)
    vf[:ch] = vp_ref[0, 0].astype(F32)
    vf[ch:] = vo_ref[0, 0].astype(F32)

    qi = lax.broadcasted_iota(I32, (ATTN_BLOCK, 2 * ATTN_BLOCK), 0)
    kj = lax.broadcasted_iota(I32, (ATTN_BLOCK, 2 * ATTN_BLOCK), 1)
    band = (kj >= qi) & (kj <= qi + ATTN_BLOCK)
    bias_all = jnp.where(band, 0.0, NEG)
    bias_own = jnp.where(band & (kj >= ATTN_BLOCK), 0.0, NEG)

    def rows(start, n, d):
        return pl.ds(start, n) if d == 1 else pl.ds(start, n, stride=d)

    def block(t, d, p_idx):
        unit = ATTN_BLOCK * d
        qs = (t // d) * unit + (t % d)
        ks = ch + qs - unit
        q = qf[rows(qs, ATTN_BLOCK, d), :].astype(BF16)
        kc = kf[rows(ks, 2 * ATTN_BLOCK, d), :].astype(BF16)
        vc = vf[rows(ks, 2 * ATTN_BLOCK, d), :].astype(BF16)
        s = lax.dot_general(q, kc, (((1,), (1,)), ((), ())), preferred_element_type=F32)
        has_prev = jnp.logical_or(c > 0, ks >= ch)
        s = s + jnp.where(has_prev, bias_all, bias_own)
        m_b = jnp.max(s, axis=-1, keepdims=True)
        p = jnp.exp(s - m_b)
        r = rows(qs, ATTN_BLOCK, d)
        full = (ATTN_BLOCK, HEAD_DIM)
        m_sc[p_idx, r, :] = jnp.broadcast_to(m_b, full)
        l_sc[p_idx, r, :] = jnp.broadcast_to(jnp.sum(p, axis=-1, keepdims=True), full)
        acc_sc[p_idx, r, :] = jnp.dot(p.astype(BF16), vc, preferred_element_type=F32)

    for p_idx, d in enumerate(DILATIONS):
        def step(t, carry, d=d, p_idx=p_idx):
            block(t, d, p_idx)
            return carry
        lax.fori_loop(0, ch // ATTN_BLOCK, step, 0, unroll=ATTN_UNROLL)

    def merge(i, carry):
        r = pl.ds(pl.multiple_of(i * MERGE_ROWS, MERGE_ROWS), MERGE_ROWS)
        ms = [m_sc[p_idx, r, :] for p_idx in range(len(DILATIONS))]
        m = functools.reduce(jnp.maximum, ms)
        num = jnp.zeros((MERGE_ROWS, HEAD_DIM), F32)
        den = jnp.zeros((MERGE_ROWS, HEAD_DIM), F32)
        for p_idx in range(len(DILATIONS)):
            a = jnp.exp(ms[p_idx] - m)
            num = num + a * acc_sc[p_idx, r, :]
            den = den + a * l_sc[p_idx, r, :]
        o_ref[0, 0, r, :] = (num / den).astype(BF16)
        return carry

    lax.fori_loop(0, ch // MERGE_ROWS, merge, 0)


def _attention(q, k, v, batch, seq):
    ch = ATTN_CHUNK
    own = pl.BlockSpec((1, 1, ch, HEAD_DIM), lambda b, h, c: (b, h, c, 0))
    prev = pl.BlockSpec((1, 1, ch, HEAD_DIM), lambda b, h, c: (b, h, jnp.maximum(c - 1, 0), 0))
    return pl.pallas_call(
        _attn_body,
        grid=(batch, N_HEADS, seq // ch),
        in_specs=[own, own, prev, own, prev],
        out_specs=own,
        out_shape=jax.ShapeDtypeStruct((batch, N_HEADS, seq, HEAD_DIM), BF16),
        scratch_shapes=[pltpu.VMEM((ch, HEAD_DIM), F32),
                        pltpu.VMEM((2 * ch, HEAD_DIM), F32),
                        pltpu.VMEM((2 * ch, HEAD_DIM), F32),
                        pltpu.VMEM((len(DILATIONS), ch, HEAD_DIM), F32),
                        pltpu.VMEM((len(DILATIONS), ch, HEAD_DIM), F32),
                        pltpu.VMEM((len(DILATIONS), ch, HEAD_DIM), F32)],
        compiler_params=_cparams(3),
        name="dilated_attn",
    )(q, k, k, v, v)


def _conv_body(um_ref, uh_ref, w_ref, b_ref, g_ref, bb_ref, o_ref, y_sc, z_sc):
    i = pl.program_id(1)
    cw = CONV_WIDTH
    uh = uh_ref[...]
    yh = uh[:, :cw] * jax.nn.sigmoid(uh[:, cw:])
    y_sc[:CONV_HALO] = jnp.where(i > 0, yh, 0.0)

    def glu(rc, carry):
        r0 = pl.multiple_of(rc * RC_CONV, RC_CONV)
        um = um_ref[pl.ds(r0, RC_CONV), :]
        y_sc[pl.ds(CONV_HALO + r0, RC_CONV), :] = um[:, :cw] * jax.nn.sigmoid(um[:, cw:])
        return carry

    lax.fori_loop(0, TC_CONV // RC_CONV, glu, 0)

    win_rows = RC_CONV + CONV_HALO
    first_off = CONV_HALO - (CONV_KERNEL - 1)

    def chunk(rc, carry):
        r0 = pl.multiple_of(rc * RC_CONV, RC_CONV)
        for cg in range(cw // CONV_COLS):
            cols = slice(cg * CONV_COLS, (cg + 1) * CONV_COLS)
            win = y_sc[pl.ds(r0, win_rows), cols]
            acc = jnp.zeros((RC_CONV, CONV_COLS), F32)
            for phase in range(SUBLANES):
                shifted = win if phase == 0 else pltpu.roll(win, win_rows - phase, 0)
                for k in range(CONV_KERNEL):
                    off = first_off + k
                    if off % SUBLANES == phase:
                        base = off - phase
                        acc = acc + shifted[base:base + RC_CONV] * w_ref[k:k + 1, cols]
            z_sc[:, cols] = acc
        yf = z_sc[...] + b_ref[...]
        mu = jnp.mean(yf, axis=-1, keepdims=True)
        yc = yf - mu
        var = jnp.mean(yc * yc, axis=-1, keepdims=True)
        yn = yc * lax.rsqrt(var + EPS) * g_ref[...] + bb_ref[...]
        o_ref[pl.ds(r0, RC_CONV), :] = (yn * jax.nn.sigmoid(yn)).astype(BF16)
        return carry

    lax.fori_loop(0, TC_CONV // RC_CONV, chunk, 0)


def _conv(u, conv_w, conv_b, ln_g, ln_b, batch, seq):
    nsb = seq // TC_CONV
    hpt = TC_CONV // CONV_HALO
    row = lambda: pl.BlockSpec((1, CONV_WIDTH), lambda b, i: (0, 0))
    return pl.pallas_call(
        _conv_body,
        grid=(batch, nsb),
        in_specs=[
            pl.BlockSpec((TC_CONV, 2 * CONV_WIDTH), lambda b, i: (b * nsb + i, 0)),
            pl.BlockSpec((CONV_HALO, 2 * CONV_WIDTH),
                         lambda b, i: (jnp.maximum((b * nsb + i) * hpt - 1, 0), 0)),
            pl.BlockSpec((CONV_KERNEL, CONV_WIDTH), lambda b, i: (0, 0)),
            row(), row(), row(),
        ],
        out_specs=pl.BlockSpec((TC_CONV, CONV_WIDTH), lambda b, i: (b * nsb + i, 0)),
        out_shape=jax.ShapeDtypeStruct((batch * seq, CONV_WIDTH), BF16),
        scratch_shapes=[pltpu.VMEM((CONV_HALO + TC_CONV, CONV_WIDTH), F32),
                        pltpu.VMEM((RC_CONV, CONV_WIDTH), F32)],
        compiler_params=_cparams(2),
        name="conformer_conv",
    )(u, u, conv_w, conv_b, ln_g, ln_b)


def _router(x1, rows, fg_ref, wr_ref, br_ref, info_ref, wts_ref, cnt_ref, carry_sc):
    tm = x1.shape[0]
    t = _rms(x1, fg_ref[...]).astype(BF16)
    logits = jnp.dot(t, wr_ref[...], preferred_element_type=F32) + br_ref[...]
    lane = lax.broadcasted_iota(I32, (tm, LANES), 1)
    lanef = lane.astype(F32)
    ninf = -jnp.inf

    def first_argmax(vals):
        top = jnp.max(vals, axis=-1, keepdims=True)
        idx = jnp.min(jnp.where(vals == top, lanef, float(LANES)), axis=-1, keepdims=True)
        return top, idx

    gl = jnp.where(lane < N_GROUPS, logits, ninf)
    gmax, gidx = first_argmax(gl)
    p_sel = 1.0 / jnp.sum(jnp.exp(gl - gmax), axis=-1, keepdims=True)
    lo = N_GROUPS + EXPERTS_PER_GROUP * gidx
    el = jnp.where((lanef >= lo) & (lanef < lo + EXPERTS_PER_GROUP), logits, ninf)
    v1, i1 = first_argmax(el)
    el2 = jnp.where(lanef == i1, ninf, el)
    v2, i2 = first_argmax(el2)
    e21 = jnp.exp(v2 - v1)
    w1 = p_sel / (1.0 + e21)
    w2 = p_sel * e21 / (1.0 + e21)

    hit1 = lanef == i1
    hit2 = lanef == i2
    mh = jnp.where(hit1 | hit2, 1.0, 0.0)
    ri = lax.broadcasted_iota(I32, (tm, tm), 0)
    ci = lax.broadcasted_iota(I32, (tm, tm), 1)
    tri = jnp.where(ci < ri, 1.0, 0.0).astype(BF16)
    before = jnp.dot(tri, mh.astype(BF16), preferred_element_type=F32) + carry_sc[...]
    rank1 = jnp.sum(jnp.where(hit1, before, 0.0), axis=-1, keepdims=True)
    rank2 = jnp.sum(jnp.where(hit2, before, 0.0), axis=-1, keepdims=True)
    carry_sc[...] = carry_sc[...] + jnp.sum(mh, axis=0, keepdims=True)
    cnt_ref[...] = carry_sc[...]

    info = jnp.where(lane == 0, i1 - N_GROUPS,
                     jnp.where(lane == 1, i2 - N_GROUPS,
                               jnp.where(lane == 2, rank1, jnp.where(lane == 3, rank2, 0.0))))
    info_ref[rows, :] = info.astype(I32)
    wts_ref[rows, :] = jnp.where(lane == 0, w1, jnp.where(lane == 1, w2, 0.0))


def _router_specs(tm):
    in_specs = [pl.BlockSpec((1, D_MODEL), lambda i: (0, 0)),
                pl.BlockSpec((D_MODEL, LANES), lambda i: (0, 0)),
                pl.BlockSpec((1, LANES), lambda i: (0, 0))]
    out_specs = [pl.BlockSpec((tm, LANES), lambda i: (i, 0)),
                 pl.BlockSpec((tm, LANES), lambda i: (i, 0)),
                 pl.BlockSpec((1, LANES), lambda i: (0, 0))]
    return in_specs, out_specs


def _router_shapes(t_tokens):
    return [jax.ShapeDtypeStruct((t_tokens, LANES), I32),
            jax.ShapeDtypeStruct((t_tokens, LANES), F32),
            jax.ShapeDtypeStruct((1, LANES), F32)]


def _zero_carry_on_first_step(carry_sc):
    @pl.when(pl.program_id(0) == 0)
    def _():
        carry_sc[...] = jnp.zeros_like(carry_sc)


def _outproj_body(attn_ref, conv_ref, x_ref, wo_ref, fg_ref, wr_ref, br_ref,
                  x1_ref, info_ref, wts_ref, cnt_ref, carry_sc):
    _zero_carry_on_first_step(carry_sc)
    for half in range(TM_OUT // ROUTER_ROWS):
        rows = pl.ds(half * ROUTER_ROWS, ROUTER_ROWS)
        cat = jnp.concatenate([attn_ref[0, h, rows, :] for h in range(N_HEADS)] + [conv_ref[rows, :]], axis=-1)
        x1 = x_ref[rows, :] + jnp.dot(cat, wo_ref[...], preferred_element_type=F32)
        x1_ref[rows, :] = x1
        _router(x1, rows, fg_ref, wr_ref, br_ref, info_ref, wts_ref, cnt_ref, carry_sc)


def _outproj_router(attn, conv, x2, wo_bf, fg, wr_bf, br, batch, seq):
    t_tokens = batch * seq
    tm = TM_OUT
    nsb = seq // tm
    r_in, r_out = _router_specs(tm)
    return pl.pallas_call(
        _outproj_body,
        grid=(t_tokens // tm,),
        in_specs=[pl.BlockSpec((1, N_HEADS, tm, HEAD_DIM), lambda i: (i // nsb, 0, i % nsb, 0)),
                  pl.BlockSpec((tm, CONV_WIDTH), lambda i: (i, 0)),
                  pl.BlockSpec((tm, D_MODEL), lambda i: (i, 0)),
                  pl.BlockSpec((D_MODEL, D_MODEL), lambda i: (0, 0))] + r_in,
        out_specs=[pl.BlockSpec((tm, D_MODEL), lambda i: (i, 0))] + r_out,
        out_shape=[jax.ShapeDtypeStruct((t_tokens, D_MODEL), F32)] + _router_shapes(t_tokens),
        scratch_shapes=[pltpu.VMEM((1, LANES), F32)],
        compiler_params=_cparams(1),
        name="outproj_router",
    )(attn, conv, x2, wo_bf, fg, wr_bf, br)


def _pool_body(x_ref, xh_ref, og_ref, pw_ref, pb_ref, ps_ref, fg_ref, wr_ref, br_ref,
               x1_ref, info_ref, wts_ref, cnt_ref, carry_sc, h_sc, *, tiles_per_seq):
    tm = x_ref.shape[0]
    ts = pl.program_id(0) % tiles_per_seq
    _zero_carry_on_first_step(carry_sc)
    h_sc[:POOL_HALO] = jnp.where(ts > 0, _rms(xh_ref[...], og_ref[...]), 0.0)
    h_sc[POOL_HALO:] = _rms(x_ref[...], og_ref[...])
    for half in range(tm // ROUTER_ROWS):
        r0 = half * ROUTER_ROWS
        rows = pl.ds(r0, ROUTER_ROWS)
        tpos = ts * tm + r0 + lax.broadcasted_iota(I32, (ROUTER_ROWS, 1), 0) + 1
        ys = []
        for g, w in enumerate(POOL_WINDOWS):
            cols = slice(g * POOL_GROUP, (g + 1) * POOL_GROUP)
            ext = h_sc[r0:r0 + POOL_HALO + ROUTER_ROWS, cols]
            tot = ext
            span = 1
            while span < w:
                tot = tot + pltpu.roll(tot, span, 0)
                span *= 2
            cur = ext[POOL_HALO:]
            cnt = jnp.minimum(tpos, w).astype(F32)
            mixed = tot[POOL_HALO:] / cnt - cur
            ys.append(jnp.dot(mixed.astype(BF16), pw_ref[g], preferred_element_type=F32))
        y = jnp.concatenate(ys, axis=-1)
        x1 = x_ref[rows, :] + (y + pb_ref[...]) * ps_ref[...]
        x1_ref[rows, :] = x1
        _router(x1, rows, fg_ref, wr_ref, br_ref, info_ref, wts_ref, cnt_ref, carry_sc)


def _pool_router(x2, og, pw_bf, pb, ps, fg, wr_bf, br, batch, seq):
    t_tokens = batch * seq
    tm = TM_OUT
    hpt = tm // POOL_HALO
    r_in, r_out = _router_specs(tm)
    vec = lambda: pl.BlockSpec((1, D_MODEL), lambda i: (0, 0))
    return pl.pallas_call(
        functools.partial(_pool_body, tiles_per_seq=seq // tm),
        grid=(t_tokens // tm,),
        in_specs=[pl.BlockSpec((tm, D_MODEL), lambda i: (i, 0)),
                  pl.BlockSpec((POOL_HALO, D_MODEL), lambda i: (jnp.maximum(i * hpt - 1, 0), 0)),
                  vec(),
                  pl.BlockSpec((len(POOL_WINDOWS), POOL_GROUP, POOL_GROUP), lambda i: (0, 0, 0)),
                  vec(), vec()] + r_in,
        out_specs=[pl.BlockSpec((tm, D_MODEL), lambda i: (i, 0))] + r_out,
        out_shape=[jax.ShapeDtypeStruct((t_tokens, D_MODEL), F32)] + _router_shapes(t_tokens),
        scratch_shapes=[pltpu.VMEM((1, LANES), F32),
                        pltpu.VMEM((POOL_HALO + tm, D_MODEL), F32)],
        compiler_params=_cparams(1),
        name="pool_router",
    )(x2, x2, og, pw_bf, pb, ps, fg, wr_bf, br)


def _bulk_wait(src, dst, sem):
    pltpu.make_async_copy(src, dst, sem).wait()


def _expert_body(te_ref, nu_ref, first_ref, nxt_ref, slot_ref, src_ref, srcn_ref, fg_ref,
                 x_hbm, wg_hbm, wu_hbm, wd_hbm, ys_ref,
                 xg, wg_f, wu_f, wd_f, wg_sc, wu_sc, wd_sc, xsems, sems, *, layer):
    j = pl.program_id(0)
    n_used = nu_ref[0]
    par = j % 2

    def gather(s_ref, buf):
        def body(r, carry):
            pltpu.make_async_copy(x_hbm.at[pl.ds(s_ref[0, 0, r], 1)], xg.at[buf, pl.ds(r, 1)],
                                  xsems.at[buf]).start()
            return carry
        lax.fori_loop(0, TM_EXP, body, 0, unroll=DMA_UNROLL)

    @pl.when(j == 0)
    def _():
        gather(src_ref, 0)

    @pl.when(j + 1 < n_used)
    def _():
        gather(srcn_ref, 1 - par)

    def weight_copies(e, s):
        return (pltpu.make_async_copy(wg_hbm.at[layer, e], wg_f.at[s], sems.at[s, 0]),
                pltpu.make_async_copy(wu_hbm.at[layer, e], wu_f.at[s], sems.at[s, 1]),
                pltpu.make_async_copy(wd_hbm.at[layer, e], wd_f.at[s], sems.at[s, 2]))

    @pl.when(j < n_used)
    def _():
        s = slot_ref[j]

        @pl.when(first_ref[j] == 1)
        def _():
            @pl.when(j == 0)
            def _():
                for cp in weight_copies(te_ref[0], 0):
                    cp.start()

            for cp in weight_copies(te_ref[j], s):
                cp.wait()

            @pl.when(nxt_ref[j] >= 0)
            def _():
                for cp in weight_copies(nxt_ref[j], 1 - s):
                    cp.start()

            wg_sc[...] = wg_f[s].astype(BF16)
            wu_sc[...] = wu_f[s].astype(BF16)
            wd_sc[...] = wd_f[s].astype(BF16)

        _bulk_wait(x_hbm.at[pl.ds(0, TM_EXP)], xg.at[par], xsems.at[par])
        t = _rms(xg[par], fg_ref[...]).astype(BF16)
        a = jnp.dot(t, wg_sc[...], preferred_element_type=F32)
        b = jnp.dot(t, wu_sc[...], preferred_element_type=F32)
        hid = (a * jax.nn.sigmoid(a) * b).astype(BF16)
        ys_ref[...] = jnp.dot(hid, wd_sc[...], preferred_element_type=F32)


def _experts(tile_expert, n_used, first, nxt, slot, src3, x1, fg, wg, wu, wd, layer):
    n_tiles = src3.shape[0]
    n_rows = n_tiles * TM_EXP
    row_idx = lambda j, te, nu, fi, nx, sl: (jnp.minimum(j, nu[0] - 1), 0)
    hbm = lambda: pl.BlockSpec(memory_space=pl.ANY)
    return pl.pallas_call(
        functools.partial(_expert_body, layer=layer),
        grid_spec=pltpu.PrefetchScalarGridSpec(
            num_scalar_prefetch=5,
            grid=(n_tiles,),
            in_specs=[pl.BlockSpec((1, 1, TM_EXP), lambda j, te, nu, fi, nx, sl: (j, 0, 0),
                                   memory_space=pltpu.SMEM),
                      pl.BlockSpec((1, 1, TM_EXP),
                                   lambda j, te, nu, fi, nx, sl: (jnp.minimum(j + 1, n_tiles - 1), 0, 0),
                                   memory_space=pltpu.SMEM),
                      pl.BlockSpec((1, D_MODEL), lambda j, te, nu, fi, nx, sl: (0, 0)),
                      hbm(), hbm(), hbm(), hbm()],
            out_specs=pl.BlockSpec((TM_EXP, D_MODEL), row_idx),
            scratch_shapes=[pltpu.VMEM((2, TM_EXP, D_MODEL), F32),
                            pltpu.VMEM((2, D_MODEL, EXPERT_HIDDEN), F32),
                            pltpu.VMEM((2, D_MODEL, EXPERT_HIDDEN), F32),
                            pltpu.VMEM((2, EXPERT_HIDDEN, D_MODEL), F32),
                            pltpu.VMEM((D_MODEL, EXPERT_HIDDEN), BF16),
                            pltpu.VMEM((D_MODEL, EXPERT_HIDDEN), BF16),
                            pltpu.VMEM((EXPERT_HIDDEN, D_MODEL), BF16),
                            pltpu.SemaphoreType.DMA((2,)),
                            pltpu.SemaphoreType.DMA((2, 3))]),
        out_shape=jax.ShapeDtypeStruct((n_rows, D_MODEL), F32),
        compiler_params=_cparams(1),
        name="moe_experts",
    )(tile_expert, n_used, first, nxt, slot, src3, src3, fg, x1, wg, wu, wd)


def _combine_body(pos_ref, posn_ref, x_ref, wts_ref, ys_hbm, o_ref, gbuf, sems, *, n_steps):
    i = pl.program_id(0)
    slot = i % 2

    def issue(p_ref, s):
        def body(r, carry):
            for k in range(2):
                pltpu.make_async_copy(ys_hbm.at[pl.ds(p_ref[0, 0, k * TC_COMB + r], 1)],
                                      gbuf.at[s, k, pl.ds(r, 1)], sems.at[s]).start()
            return carry
        lax.fori_loop(0, TC_COMB, body, 0, unroll=DMA_UNROLL)

    @pl.when(i == 0)
    def _():
        issue(pos_ref, 0)

    @pl.when(i + 1 < n_steps)
    def _():
        issue(posn_ref, 1 - slot)

    for k in range(2):
        _bulk_wait(ys_hbm.at[pl.ds(0, TC_COMB)], gbuf.at[slot, k], sems.at[slot])
    w = wts_ref[...]
    o_ref[...] = x_ref[...] + w[:, 0:1] * gbuf[slot, 0] + w[:, 1:2] * gbuf[slot, 1]


def _combine(pos3, x1, wts, ys):
    t_tokens = x1.shape[0]
    n_steps = t_tokens // TC_COMB
    return pl.pallas_call(
        functools.partial(_combine_body, n_steps=n_steps),
        grid=(n_steps,),
        in_specs=[pl.BlockSpec((1, 1, 2 * TC_COMB), lambda i: (i, 0, 0), memory_space=pltpu.SMEM),
                  pl.BlockSpec((1, 1, 2 * TC_COMB), lambda i: (jnp.minimum(i + 1, n_steps - 1), 0, 0),
                               memory_space=pltpu.SMEM),
                  pl.BlockSpec((TC_COMB, D_MODEL), lambda i: (i, 0)),
                  pl.BlockSpec((TC_COMB, LANES), lambda i: (i, 0)),
                  pl.BlockSpec(memory_space=pl.ANY)],
        out_specs=pl.BlockSpec((TC_COMB, D_MODEL), lambda i: (i, 0)),
        out_shape=jax.ShapeDtypeStruct((t_tokens, D_MODEL), F32),
        scratch_shapes=[pltpu.VMEM((2, 2, TC_COMB, D_MODEL), F32),
                        pltpu.SemaphoreType.DMA((2,))],
        compiler_params=_cparams(1),
        name="moe_combine",
    )(pos3, pos3, x1, wts, ys)


def _moe(x1, info, wts, counts, fg, wg, wu, wd, layer):
    t_tokens = x1.shape[0]
    n_tiles = 2 * t_tokens // TM_EXP + N_EXPERTS
    cnt = counts[0, N_GROUPS:N_GROUPS + N_EXPERTS].astype(I32)
    padded = (cnt + TM_EXP - 1) // TM_EXP * TM_EXP
    seg_end = jnp.cumsum(padded)
    seg_start = seg_end - padded
    n_used = seg_end[-1] // TM_EXP
    tile_id = jnp.arange(n_tiles, dtype=I32)
    tile_row = jnp.minimum(tile_id, n_used - 1) * TM_EXP
    tile_expert = jnp.sum((tile_row[:, None] >= seg_end[None, :]).astype(I32), axis=1)
    ids = jnp.arange(N_EXPERTS, dtype=I32)

    def slot_rows(k):
        start = jnp.sum(jnp.where(info[:, k][:, None] == ids, seg_start, 0), axis=-1)
        return start + info[:, 2 + k]

    def per_tile(n):
        return jnp.concatenate([p.reshape(t_tokens // n, 1, n) for p in pos], axis=2)

    pos = [slot_rows(0), slot_rows(1)]
    tok = jnp.arange(t_tokens, dtype=I32)
    src = jnp.zeros((n_tiles * TM_EXP,), I32)
    for p in pos:
        src = src.at[p].set(tok, unique_indices=True)
    src3 = src.reshape(n_tiles, 1, TM_EXP)
    first = jnp.concatenate([jnp.ones((1,), I32), (tile_expert[1:] != tile_expert[:-1]).astype(I32)])
    first = jnp.where(tile_id < n_used, first, 0)
    slot = (jnp.cumsum(first) - 1) % 2
    later = (ids[None, :] > ids[:, None]) & (cnt[None, :] > 0)
    next_of = jnp.min(jnp.where(later, ids[None, :], N_EXPERTS), axis=1)
    next_of = jnp.where(next_of < N_EXPERTS, next_of, -1)
    nxt = jnp.sum(jnp.where(tile_expert[:, None] == ids[None, :], next_of[None, :], 0), axis=1)
    ys = _experts(tile_expert, n_used.reshape(1), first, nxt, slot.astype(I32), src3, x1, fg, wg, wu, wd, layer)
    return _combine(per_tile(TC_COMB), x1, wts, ys)


def _router_params(w_rg, b_rg, w_re, b_re):
    pad = LANES - N_GROUPS - N_EXPERTS
    wr = jnp.concatenate([w_rg, w_re, jnp.zeros((D_MODEL, pad), F32)], axis=1).astype(BF16)
    br = jnp.concatenate([b_rg, b_re, jnp.zeros((pad,), F32)]).reshape(1, LANES)
    return wr, br


def kernel(x, positions, even_norm, w_in, q_norm, k_norm, conv_w, conv_b, conv_ln_g, conv_ln_b, w_out,
           odd_norm, pool_w, pool_b, pool_scale, ffn_norm, w_router_group, b_router_group,
           w_router_expert, b_router_expert, w_expert_gate, w_expert_up, w_expert_down):
    batch, seq, d = x.shape
    depth = ffn_norm.shape[0]
    assert d == D_MODEL and seq % ATTN_CHUNK == 0 and seq % TM_OUT == 0
    t_tokens = batch * seq
    half = HEAD_DIM // 2
    inv_freq = jnp.float32(ROPE_THETA) ** (-jnp.arange(half, dtype=F32) / half)
    invf2 = jnp.concatenate([inv_freq, inv_freq]).reshape(1, HEAD_DIM)
    pos2 = positions.reshape(t_tokens, 1)
    row = lambda a: a.reshape(1, -1)

    x2 = x.reshape(t_tokens, d)
    for i in range(depth):
        j = i // 2
        wr, br = _router_params(w_router_group[i], b_router_group[i], w_router_expert[i], b_router_expert[i])
        fg = row(ffn_norm[i])
        if i % 2 == 0:
            q, k, v, u = _inproj(x2, pos2, row(even_norm[j]), w_in[j].astype(BF16), row(q_norm[j]),
                                 row(k_norm[j]), invf2, batch, seq)
            attn = _attention(q, k, v, batch, seq)
            conv = _conv(u, conv_w[j], row(conv_b[j]), row(conv_ln_g[j]), row(conv_ln_b[j]), batch, seq)
            x1, info, wts, counts = _outproj_router(attn, conv, x2, w_out[j].astype(BF16), fg, wr, br,
                                                    batch, seq)
        else:
            x1, info, wts, counts = _pool_router(x2, row(odd_norm[j]), pool_w[j].astype(BF16),
                                                 row(pool_b[j]), row(pool_scale[j]), fg, wr, br, batch, seq)
        x2 = _moe(x1, info, wts, counts, fg, w_expert_gate, w_expert_up, w_expert_down, i)
    return x2.reshape(batch, seq, d)
```

```python
import functools

import jax
import jax.numpy as jnp
from jax import lax
from jax.experimental import pallas as pl
from jax.experimental.pallas import tpu as pltpu

F32 = jnp.float32
BF16 = jnp.bfloat16
I32 = jnp.int32

D_MODEL = 2048
HEAD_DIM = 128
N_HEADS = 8
ATTN_WIDTH = N_HEADS * HEAD_DIM
ATTN_BLOCK = 128
DILATIONS = (1, 4, 16)
ATTN_CHUNK = ATTN_BLOCK * max(DILATIONS)
ATTN_UNROLL = 8
MERGE_ROWS = 256
CONV_WIDTH = D_MODEL - ATTN_WIDTH
CONV_KERNEL = 31
CONV_HALO = 32
IN_PROJ_WIDTH = 3 * ATTN_WIDTH + 2 * CONV_WIDTH
POOL_WINDOWS = (2, 4, 8, 16)
POOL_GROUP = D_MODEL // len(POOL_WINDOWS)
POOL_HALO = 16
N_GROUPS = 4
EXPERTS_PER_GROUP = 8
N_EXPERTS = N_GROUPS * EXPERTS_PER_GROUP
EXPERT_HIDDEN = 512
ROPE_THETA = 10000.0
EPS = 1e-6
LANES = 128
NEG = -1e30

TM_PROJ = 256
TN_PROJ = 1024
TM_OUT = 512
ROUTER_ROWS = 256
TC_CONV = 512
RC_CONV = 64
CONV_COLS = 128
SUBLANES = 8
TM_EXP = 256
TD_DISP = 512
TC_COMB = 256
DMA_UNROLL = 8
VMEM_LIMIT = 56 * 1024 * 1024


def _cparams(n_axes):
    return pltpu.CompilerParams(dimension_semantics=("arbitrary",) * n_axes,
                                vmem_limit_bytes=VMEM_LIMIT)


def _rms(x, g):
    return x * lax.rsqrt(jnp.mean(x * x, axis=-1, keepdims=True) + EPS) * g


def _inproj_body(x_ref, pos_ref, g_ref, w_ref, qn_ref, kn_ref, invf_ref, q_ref, k_ref, v_ref, u_ref, hn_sc):
    hn_sc[...] = _rms(x_ref[...], g_ref[...]).astype(BF16)
    ang = pos_ref[...].astype(F32) * invf_ref[...]
    lane = lax.broadcasted_iota(I32, ang.shape, 1)
    cs = jnp.cos(ang)
    sn = jnp.where(lane < HEAD_DIM // 2, -1.0, 1.0) * jnp.sin(ang)

    def section(n):
        return jnp.dot(hn_sc[...], w_ref[:, n * TN_PROJ:(n + 1) * TN_PROJ], preferred_element_type=F32)

    def qk_heads(acc, norm_ref, out_ref, scale):
        for h in range(N_HEADS):
            y = _rms(acc[:, h * HEAD_DIM:(h + 1) * HEAD_DIM], norm_ref[...])
            y = y * cs + pltpu.roll(y, HEAD_DIM // 2, 1) * sn
            out_ref[0, h] = (y * scale).astype(BF16)

    qk_heads(section(0), qn_ref, q_ref, HEAD_DIM ** -0.5)
    qk_heads(section(1), kn_ref, k_ref, 1.0)
    acc = section(2)
    for h in range(N_HEADS):
        v_ref[0, h] = acc[:, h * HEAD_DIM:(h + 1) * HEAD_DIM].astype(BF16)
    u_ref[:, :TN_PROJ] = section(3)
    u_ref[:, TN_PROJ:] = section(4)


def _inproj(x2, pos2, g, w_bf, qn, kn, invf2, batch, seq):
    t_tokens = batch * seq
    nsb = seq // TM_PROJ
    head_spec = pl.BlockSpec((1, N_HEADS, TM_PROJ, HEAD_DIM), lambda i: (i // nsb, 0, i % nsb, 0))
    head_shape = jax.ShapeDtypeStruct((batch, N_HEADS, seq, HEAD_DIM), BF16)
    return pl.pallas_call(
        _inproj_body,
        grid=(t_tokens // TM_PROJ,),
        in_specs=[
            pl.BlockSpec((TM_PROJ, D_MODEL), lambda i: (i, 0)),
            pl.BlockSpec((TM_PROJ, 1), lambda i: (i, 0)),
            pl.BlockSpec((1, D_MODEL), lambda i: (0, 0)),
            pl.BlockSpec((D_MODEL, IN_PROJ_WIDTH), lambda i: (0, 0), pipeline_mode=pl.Buffered(1)),
            pl.BlockSpec((1, HEAD_DIM), lambda i: (0, 0)),
            pl.BlockSpec((1, HEAD_DIM), lambda i: (0, 0)),
            pl.BlockSpec((1, HEAD_DIM), lambda i: (0, 0)),
        ],
        out_specs=[head_spec, head_spec, head_spec,
                   pl.BlockSpec((TM_PROJ, 2 * CONV_WIDTH), lambda i: (i, 0))],
        out_shape=[head_shape, head_shape, head_shape,
                   jax.ShapeDtypeStruct((t_tokens, 2 * CONV_WIDTH), F32)],
        scratch_shapes=[pltpu.VMEM((TM_PROJ, D_MODEL), BF16)],
        compiler_params=_cparams(1),
        name="inproj",
    )(x2, pos2, g, w_bf, qn, kn, invf2)


def _attn_body(q_ref, ko_ref, kp_ref, vo_ref, vp_ref, o_ref, qf, kf, vf, m_sc, l_sc, acc_sc):
    c = pl.program_id(2)
    ch = ATTN_CHUNK
    qf[...] = q_ref[0, 0].astype(F32)
    kf[:ch] = kp_ref[0, 0].astype(F32)
    kf[ch:] = ko_ref[0, 0].astype(F32)
    vf[:ch] = vp_ref[0, 0].astype(F32)
    vf[ch:] = vo_ref[0, 0].astype(F32)

    qi = lax.broadcasted_iota(I32, (ATTN_BLOCK, 2 * ATTN_BLOCK), 0)
    kj = lax.broadcasted_iota(I32, (ATTN_BLOCK, 2 * ATTN_BLOCK), 1)
    band = (kj >= qi) & (kj <= qi + ATTN_BLOCK)
    bias_all = jnp.where(band, 0.0, NEG)
    bias_own = jnp.where(band & (kj >= ATTN_BLOCK), 0.0, NEG)

    def rows(start, n, d):
        return pl.ds(start, n) if d == 1 else pl.ds(start, n, stride=d)

    def block(t, d, p_idx):
        unit = ATTN_BLOCK * d
        qs = (t // d) * unit + (t % d)
        ks = ch + qs - unit
        q = qf[rows(qs, ATTN_BLOCK, d), :].astype(BF16)
        kc = kf[rows(ks, 2 * ATTN_BLOCK, d), :].astype(BF16)
        vc = vf[rows(ks, 2 * ATTN_BLOCK, d), :].astype(BF16)
        s = lax.dot_general(q, kc, (((1,), (1,)), ((), ())), preferred_element_type=F32)
        has_prev = jnp.logical_or(c > 0, ks >= ch)
        s = s + jnp.where(has_prev, bias_all, bias_own)
        m_b = jnp.max(s, axis=-1, keepdims=True)
        p = jnp.exp(s - m_b)
        r = rows(qs, ATTN_BLOCK, d)
        full = (ATTN_BLOCK, HEAD_DIM)
        m_sc[p_idx, r, :] = jnp.broadcast_to(m_b, full)
        l_sc[p_idx, r, :] = jnp.broadcast_to(jnp.sum(p, axis=-1, keepdims=True), full)
        acc_sc[p_idx, r, :] = jnp.dot(p.astype(BF16), vc, preferred_element_type=F32)

    for p_idx, d in enumerate(DILATIONS):
        def step(t, carry, d=d, p_idx=p_idx):
            block(t, d, p_idx)
            return carry
        lax.fori_loop(0, ch // ATTN_BLOCK, step, 0, unroll=ATTN_UNROLL)

    def merge(i, carry):
        r = pl.ds(pl.multiple_of(i * MERGE_ROWS, MERGE_ROWS), MERGE_ROWS)
        ms = [m_sc[p_idx, r, :] for p_idx in range(len(DILATIONS))]
        m = functools.reduce(jnp.maximum, ms)
        num = jnp.zeros((MERGE_ROWS, HEAD_DIM), F32)
        den = jnp.zeros((MERGE_ROWS, HEAD_DIM), F32)
        for p_idx in range(len(DILATIONS)):
            a = jnp.exp(ms[p_idx] - m)
            num = num + a * acc_sc[p_idx, r, :]
            den = den + a * l_sc[p_idx, r, :]
        o_ref[0, 0, r, :] = (num / den).astype(BF16)
        return carry

    lax.fori_loop(0, ch // MERGE_ROWS, merge, 0)


def _attention(q, k, v, batch, seq):
    ch = ATTN_CHUNK
    own = pl.BlockSpec((1, 1, ch, HEAD_DIM), lambda b, h, c: (b, h, c, 0))
    prev = pl.BlockSpec((1, 1, ch, HEAD_DIM), lambda b, h, c: (b, h, jnp.maximum(c - 1, 0), 0))
    return pl.pallas_call(
        _attn_body,
        grid=(batch, N_HEADS, seq // ch),
        in_specs=[own, own, prev, own, prev],
        out_specs=own,
        out_shape=jax.ShapeDtypeStruct((batch, N_HEADS, seq, HEAD_DIM), BF16),
        scratch_shapes=[pltpu.VMEM((ch, HEAD_DIM), F32),
                        pltpu.VMEM((2 * ch, HEAD_DIM), F32),
                        pltpu.VMEM((2 * ch, HEAD_DIM), F32),
                        pltpu.VMEM((len(DILATIONS), ch, HEAD_DIM), F32),
                        pltpu.VMEM((len(DILATIONS), ch, HEAD_DIM), F32),
                        pltpu.VMEM((len(DILATIONS), ch, HEAD_DIM), F32)],
        compiler_params=_cparams(3),
        name="dilated_attn",
    )(q, k, k, v, v)


def _conv_body(um_ref, uh_ref, w_ref, b_ref, g_ref, bb_ref, o_ref, y_sc, z_sc):
    i = pl.program_id(1)
    cw = CONV_WIDTH
    uh = uh_ref[...]
    yh = uh[:, :cw] * jax.nn.sigmoid(uh[:, cw:])
    y_sc[:CONV_HALO] = jnp.where(i > 0, yh, 0.0)

    def glu(rc, carry):
        r0 = pl.multiple_of(rc * RC_CONV, RC_CONV)
        um = um_ref[pl.ds(r0, RC_CONV), :]
        y_sc[pl.ds(CONV_HALO + r0, RC_CONV), :] = um[:, :cw] * jax.nn.sigmoid(um[:, cw:])
        return carry

    lax.fori_loop(0, TC_CONV // RC_CONV, glu, 0)

    win_rows = RC_CONV + CONV_HALO
    first_off = CONV_HALO - (CONV_KERNEL - 1)

    def chunk(rc, carry):
        r0 = pl.multiple_of(rc * RC_CONV, RC_CONV)
        for cg in range(cw // CONV_COLS):
            cols = slice(cg * CONV_COLS, (cg + 1) * CONV_COLS)
            win = y_sc[pl.ds(r0, win_rows), cols]
            acc = jnp.zeros((RC_CONV, CONV_COLS), F32)
            for phase in range(SUBLANES):
                shifted = win if phase == 0 else pltpu.roll(win, win_rows - phase, 0)
                for k in range(CONV_KERNEL):
                    off = first_off + k
                    if off % SUBLANES == phase:
                        base = off - phase
                        acc = acc + shifted[base:base + RC_CONV] * w_ref[k:k + 1, cols]
            z_sc[:, cols] = acc
        yf = z_sc[...] + b_ref[...]
        mu = jnp.mean(yf, axis=-1, keepdims=True)
        yc = yf - mu
        var = jnp.mean(yc * yc, axis=-1, keepdims=True)
        yn = yc * lax.rsqrt(var + EPS) * g_ref[...] + bb_ref[...]
        o_ref[pl.ds(r0, RC_CONV), :] = (yn * jax.nn.sigmoid(yn)).astype(BF16)
        return carry

    lax.fori_loop(0, TC_CONV // RC_CONV, chunk, 0)


def _conv(u, conv_w, conv_b, ln_g, ln_b, batch, seq):
    nsb = seq // TC_CONV
    hpt = TC_CONV // CONV_HALO
    row = lambda: pl.BlockSpec((1, CONV_WIDTH), lambda b, i: (0, 0))
    return pl.pallas_call(
        _conv_body,
        grid=(batch, nsb),
        in_specs=[
            pl.BlockSpec((TC_CONV, 2 * CONV_WIDTH), lambda b, i: (b * nsb + i, 0)),
            pl.BlockSpec((CONV_HALO, 2 * CONV_WIDTH),
                         lambda b, i: (jnp.maximum((b * nsb + i) * hpt - 1, 0), 0)),
            pl.BlockSpec((CONV_KERNEL, CONV_WIDTH), lambda b, i: (0, 0)),
            row(), row(), row(),
        ],
        out_specs=pl.BlockSpec((TC_CONV, CONV_WIDTH), lambda b, i: (b * nsb + i, 0)),
        out_shape=jax.ShapeDtypeStruct((batch * seq, CONV_WIDTH), BF16),
        scratch_shapes=[pltpu.VMEM((CONV_HALO + TC_CONV, CONV_WIDTH), F32),
                        pltpu.VMEM((RC_CONV, CONV_WIDTH), F32)],
        compiler_params=_cparams(2),
        name="conformer_conv",
    )(u, u, conv_w, conv_b, ln_g, ln_b)


def _router(x1, rows, fg_ref, wr_ref, br_ref, info_ref, wts_ref, cnt_ref, carry_sc):
    tm = x1.shape[0]
    t = _rms(x1, fg_ref[...]).astype(BF16)
    logits = jnp.dot(t, wr_ref[...], preferred_element_type=F32) + br_ref[...]
    lane = lax.broadcasted_iota(I32, (tm, LANES), 1)
    lanef = lane.astype(F32)
    ninf = -jnp.inf

    def first_argmax(vals):
        top = jnp.max(vals, axis=-1, keepdims=True)
        idx = jnp.min(jnp.where(vals == top, lanef, float(LANES)), axis=-1, keepdims=True)
        return top, idx

    gl = jnp.where(lane < N_GROUPS, logits, ninf)
    gmax, gidx = first_argmax(gl)
    p_sel = 1.0 / jnp.sum(jnp.exp(gl - gmax), axis=-1, keepdims=True)
    lo = N_GROUPS + EXPERTS_PER_GROUP * gidx
    el = jnp.where((lanef >= lo) & (lanef < lo + EXPERTS_PER_GROUP), logits, ninf)
    v1, i1 = first_argmax(el)
    el2 = jnp.where(lanef == i1, ninf, el)
    v2, i2 = first_argmax(el2)
    e21 = jnp.exp(v2 - v1)
    w1 = p_sel / (1.0 + e21)
    w2 = p_sel * e21 / (1.0 + e21)

    hit1 = lanef == i1
    hit2 = lanef == i2
    mh = jnp.where(hit1 | hit2, 1.0, 0.0)
    ri = lax.broadcasted_iota(I32, (tm, tm), 0)
    ci = lax.broadcasted_iota(I32, (tm, tm), 1)
    tri = jnp.where(ci < ri, 1.0, 0.0).astype(BF16)
    before = jnp.dot(tri, mh.astype(BF16), preferred_element_type=F32) + carry_sc[...]
    rank1 = jnp.sum(jnp.where(hit1, before, 0.0), axis=-1, keepdims=True)
    rank2 = jnp.sum(jnp.where(hit2, before, 0.0), axis=-1, keepdims=True)
    carry_sc[...] = carry_sc[...] + jnp.sum(mh, axis=0, keepdims=True)
    cnt_ref[...] = carry_sc[...]

    info = jnp.where(lane == 0, i1 - N_GROUPS,
                     jnp.where(lane == 1, i2 - N_GROUPS,
                               jnp.where(lane == 2, rank1, jnp.where(lane == 3, rank2, 0.0))))
    info_ref[rows, :] = info.astype(I32)
    wts_ref[rows, :] = jnp.where(lane == 0, w1, jnp.where(lane == 1, w2, 0.0))


def _router_specs(tm):
    in_specs = [pl.BlockSpec((1, D_MODEL), lambda i: (0, 0)),
                pl.BlockSpec((D_MODEL, LANES), lambda i: (0, 0)),
                pl.BlockSpec((1, LANES), lambda i: (0, 0))]
    out_specs = [pl.BlockSpec((tm, LANES), lambda i: (i, 0)),
                 pl.BlockSpec((tm, LANES), lambda i: (i, 0)),
                 pl.BlockSpec((1, LANES), lambda i: (0, 0))]
    return in_specs, out_specs


def _router_shapes(t_tokens):
    return [jax.ShapeDtypeStruct((t_tokens, LANES), I32),
            jax.ShapeDtypeStruct((t_tokens, LANES), F32),
            jax.ShapeDtypeStruct((1, LANES), F32)]


def _zero_carry_on_first_step(carry_sc):
    @pl.when(pl.program_id(0) == 0)
    def _():
        carry_sc[...] = jnp.zeros_like(carry_sc)


def _outproj_body(attn_ref, conv_ref, x_ref, wo_ref, fg_ref, wr_ref, br_ref,
                  x1_ref, info_ref, wts_ref, cnt_ref, carry_sc):
    _zero_carry_on_first_step(carry_sc)
    for half in range(TM_OUT // ROUTER_ROWS):
        rows = pl.ds(half * ROUTER_ROWS, ROUTER_ROWS)
        cat = jnp.concatenate([attn_ref[0, h, rows, :] for h in range(N_HEADS)] + [conv_ref[rows, :]], axis=-1)
        x1 = x_ref[rows, :] + jnp.dot(cat, wo_ref[...], preferred_element_type=F32)
        x1_ref[rows, :] = x1
        _router(x1, rows, fg_ref, wr_ref, br_ref, info_ref, wts_ref, cnt_ref, carry_sc)


def _outproj_router(attn, conv, x2, wo_bf, fg, wr_bf, br, batch, seq):
    t_tokens = batch * seq
    tm = TM_OUT
    nsb = seq // tm
    r_in, r_out = _router_specs(tm)
    return pl.pallas_call(
        _outproj_body,
        grid=(t_tokens // tm,),
        in_specs=[pl.BlockSpec((1, N_HEADS, tm, HEAD_DIM), lambda i: (i // nsb, 0, i % nsb, 0)),
                  pl.BlockSpec((tm, CONV_WIDTH), lambda i: (i, 0)),
                  pl.BlockSpec((tm, D_MODEL), lambda i: (i, 0)),
                  pl.BlockSpec((D_MODEL, D_MODEL), lambda i: (0, 0))] + r_in,
        out_specs=[pl.BlockSpec((tm, D_MODEL), lambda i: (i, 0))] + r_out,
        out_shape=[jax.ShapeDtypeStruct((t_tokens, D_MODEL), F32)] + _router_shapes(t_tokens),
        scratch_shapes=[pltpu.VMEM((1, LANES), F32)],
        compiler_params=_cparams(1),
        name="outproj_router",
    )(attn, conv, x2, wo_bf, fg, wr_bf, br)


def _pool_body(x_ref, xh_ref, og_ref, pw_ref, pb_ref, ps_ref, fg_ref, wr_ref, br_ref,
               x1_ref, info_ref, wts_ref, cnt_ref, carry_sc, h_sc, *, tiles_per_seq):
    tm = x_ref.shape[0]
    ts = pl.program_id(0) % tiles_per_seq
    _zero_carry_on_first_step(carry_sc)
    h_sc[:POOL_HALO] = jnp.where(ts > 0, _rms(xh_ref[...], og_ref[...]), 0.0)
    h_sc[POOL_HALO:] = _rms(x_ref[...], og_ref[...])
    for half in range(tm // ROUTER_ROWS):
        r0 = half * ROUTER_ROWS
        rows = pl.ds(r0, ROUTER_ROWS)
        tpos = ts * tm + r0 + lax.broadcasted_iota(I32, (ROUTER_ROWS, 1), 0) + 1
        ys = []
        for g, w in enumerate(POOL_WINDOWS):
            cols = slice(g * POOL_GROUP, (g + 1) * POOL_GROUP)
            ext = h_sc[r0:r0 + POOL_HALO + ROUTER_ROWS, cols]
            tot = ext
            span = 1
            while span < w:
                tot = tot + pltpu.roll(tot, span, 0)
                span *= 2
            cur = ext[POOL_HALO:]
            cnt = jnp.minimum(tpos, w).astype(F32)
            mixed = tot[POOL_HALO:] / cnt - cur
            ys.append(jnp.dot(mixed.astype(BF16), pw_ref[g], preferred_element_type=F32))
        y = jnp.concatenate(ys, axis=-1)
        x1 = x_ref[rows, :] + (y + pb_ref[...]) * ps_ref[...]
        x1_ref[rows, :] = x1
        _router(x1, rows, fg_ref, wr_ref, br_ref, info_ref, wts_ref, cnt_ref, carry_sc)


def _pool_router(x2, og, pw_bf, pb, ps, fg, wr_bf, br, batch, seq):
    t_tokens = batch * seq
    tm = TM_OUT
    hpt = tm // POOL_HALO
    r_in, r_out = _router_specs(tm)
    vec = lambda: pl.BlockSpec((1, D_MODEL), lambda i: (0, 0))
    return pl.pallas_call(
        functools.partial(_pool_body, tiles_per_seq=seq // tm),
        grid=(t_tokens // tm,),
        in_specs=[pl.BlockSpec((tm, D_MODEL), lambda i: (i, 0)),
                  pl.BlockSpec((POOL_HALO, D_MODEL), lambda i: (jnp.maximum(i * hpt - 1, 0), 0)),
                  vec(),
                  pl.BlockSpec((len(POOL_WINDOWS), POOL_GROUP, POOL_GROUP), lambda i: (0, 0, 0)),
                  vec(), vec()] + r_in,
        out_specs=[pl.BlockSpec((tm, D_MODEL), lambda i: (i, 0))] + r_out,
        out_shape=[jax.ShapeDtypeStruct((t_tokens, D_MODEL), F32)] + _router_shapes(t_tokens),
        scratch_shapes=[pltpu.VMEM((1, LANES), F32),
                        pltpu.VMEM((POOL_HALO + tm, D_MODEL), F32)],
        compiler_params=_cparams(1),
        name="pool_router",
    )(x2, x2, og, pw_bf, pb, ps, fg, wr_bf, br)


def _bulk_wait(src, dst, sem):
    pltpu.make_async_copy(src, dst, sem).wait()


def _dispatch_body(pos_ref, x_ref, xs_hbm, sem):
    def issue(r, carry):
        for k in range(2):
            pltpu.make_async_copy(x_ref.at[pl.ds(r, 1)],
                                  xs_hbm.at[pl.ds(pos_ref[0, 0, k * TD_DISP + r], 1)], sem).start()
        return carry

    lax.fori_loop(0, TD_DISP, issue, 0, unroll=DMA_UNROLL)
    for k in range(2):
        _bulk_wait(x_ref, xs_hbm.at[pl.ds(0, TD_DISP)], sem)


def _dispatch(pos3, x1, n_rows):
    t_tokens = x1.shape[0]
    return pl.pallas_call(
        _dispatch_body,
        grid=(t_tokens // TD_DISP,),
        in_specs=[pl.BlockSpec((1, 1, 2 * TD_DISP), lambda i: (i, 0, 0), memory_space=pltpu.SMEM),
                  pl.BlockSpec((TD_DISP, D_MODEL), lambda i: (i, 0))],
        out_specs=pl.BlockSpec(memory_space=pl.ANY),
        out_shape=jax.ShapeDtypeStruct((n_rows, D_MODEL), F32),
        scratch_shapes=[pltpu.SemaphoreType.DMA(())],
        compiler_params=_cparams(1),
        name="moe_dispatch",
    )(pos3, x1)


def _expert_body(te_ref, nu_ref, first_ref, nxt_ref, slot_ref, xs_ref, fg_ref, wg_hbm, wu_hbm, wd_hbm, ys_ref,
                 wg_f, wu_f, wd_f, wg_sc, wu_sc, wd_sc, sems, *, layer):
    j = pl.program_id(0)

    def weight_copies(e, s):
        return (pltpu.make_async_copy(wg_hbm.at[layer, e], wg_f.at[s], sems.at[s, 0]),
                pltpu.make_async_copy(wu_hbm.at[layer, e], wu_f.at[s], sems.at[s, 1]),
                pltpu.make_async_copy(wd_hbm.at[layer, e], wd_f.at[s], sems.at[s, 2]))

    @pl.when(j < nu_ref[0])
    def _():
        s = slot_ref[j]

        @pl.when(first_ref[j] == 1)
        def _():
            @pl.when(j == 0)
            def _():
                for cp in weight_copies(te_ref[0], 0):
                    cp.start()

            for cp in weight_copies(te_ref[j], s):
                cp.wait()

            @pl.when(nxt_ref[j] >= 0)
            def _():
                for cp in weight_copies(nxt_ref[j], 1 - s):
                    cp.start()

            wg_sc[...] = wg_f[s].astype(BF16)
            wu_sc[...] = wu_f[s].astype(BF16)
            wd_sc[...] = wd_f[s].astype(BF16)

        t = _rms(xs_ref[...], fg_ref[...]).astype(BF16)
        a = jnp.dot(t, wg_sc[...], preferred_element_type=F32)
        b = jnp.dot(t, wu_sc[...], preferred_element_type=F32)
        hid = (a * jax.nn.sigmoid(a) * b).astype(BF16)
        ys_ref[...] = jnp.dot(hid, wd_sc[...], preferred_element_type=F32)


def _experts(tile_expert, n_used, first, nxt, slot, xs, fg, wg, wu, wd, layer):
    n_rows = xs.shape[0]
    n_tiles = n_rows // TM_EXP
    row_idx = lambda j, te, nu, fi, nx, sl: (jnp.minimum(j, nu[0] - 1), 0)
    hbm = lambda: pl.BlockSpec(memory_space=pl.ANY)
    return pl.pallas_call(
        functools.partial(_expert_body, layer=layer),
        grid_spec=pltpu.PrefetchScalarGridSpec(
            num_scalar_prefetch=5,
            grid=(n_tiles,),
            in_specs=[pl.BlockSpec((TM_EXP, D_MODEL), row_idx),
                      pl.BlockSpec((1, D_MODEL), lambda j, te, nu, fi, nx, sl: (0, 0)),
                      hbm(), hbm(), hbm()],
            out_specs=pl.BlockSpec((TM_EXP, D_MODEL), row_idx),
            scratch_shapes=[pltpu.VMEM((2, D_MODEL, EXPERT_HIDDEN), F32),
                            pltpu.VMEM((2, D_MODEL, EXPERT_HIDDEN), F32),
                            pltpu.VMEM((2, EXPERT_HIDDEN, D_MODEL), F32),
                            pltpu.VMEM((D_MODEL, EXPERT_HIDDEN), BF16),
                            pltpu.VMEM((D_MODEL, EXPERT_HIDDEN), BF16),
                            pltpu.VMEM((EXPERT_HIDDEN, D_MODEL), BF16),
                            pltpu.SemaphoreType.DMA((2, 3))]),
        out_shape=jax.ShapeDtypeStruct((n_rows, D_MODEL), F32),
        compiler_params=_cparams(1),
        name="moe_experts",
    )(tile_expert, n_used, first, nxt, slot, xs, fg, wg, wu, wd)


def _combine_body(pos_ref, posn_ref, x_ref, wts_ref, ys_hbm, o_ref, gbuf, sems, *, n_steps):
    i = pl.program_id(0)
    slot = i % 2

    def issue(p_ref, s):
        def body(r, carry):
            for k in range(2):
                pltpu.make_async_copy(ys_hbm.at[pl.ds(p_ref[0, 0, k * TC_COMB + r], 1)],
                                      gbuf.at[s, k, pl.ds(r, 1)], sems.at[s]).start()
            return carry
        lax.fori_loop(0, TC_COMB, body, 0, unroll=DMA_UNROLL)

    @pl.when(i == 0)
    def _():
        issue(pos_ref, 0)

    @pl.when(i + 1 < n_steps)
    def _():
        issue(posn_ref, 1 - slot)

    for k in range(2):
        _bulk_wait(ys_hbm.at[pl.ds(0, TC_COMB)], gbuf.at[slot, k], sems.at[slot])
    w = wts_ref[...]
    o_ref[...] = x_ref[...] + w[:, 0:1] * gbuf[slot, 0] + w[:, 1:2] * gbuf[slot, 1]


def _combine(pos3, x1, wts, ys):
    t_tokens = x1.shape[0]
    n_steps = t_tokens // TC_COMB
    return pl.pallas_call(
        functools.partial(_combine_body, n_steps=n_steps),
        grid=(n_steps,),
        in_specs=[pl.BlockSpec((1, 1, 2 * TC_COMB), lambda i: (i, 0, 0), memory_space=pltpu.SMEM),
                  pl.BlockSpec((1, 1, 2 * TC_COMB), lambda i: (jnp.minimum(i + 1, n_steps - 1), 0, 0),
                               memory_space=pltpu.SMEM),
                  pl.BlockSpec((TC_COMB, D_MODEL), lambda i: (i, 0)),
                  pl.BlockSpec((TC_COMB, LANES), lambda i: (i, 0)),
                  pl.BlockSpec(memory_space=pl.ANY)],
        out_specs=pl.BlockSpec((TC_COMB, D_MODEL), lambda i: (i, 0)),
        out_shape=jax.ShapeDtypeStruct((t_tokens, D_MODEL), F32),
        scratch_shapes=[pltpu.VMEM((2, 2, TC_COMB, D_MODEL), F32),
                        pltpu.SemaphoreType.DMA((2,))],
        compiler_params=_cparams(1),
        name="moe_combine",
    )(pos3, pos3, x1, wts, ys)


def _moe(x1, info, wts, counts, fg, wg, wu, wd, layer):
    t_tokens = x1.shape[0]
    n_tiles = 2 * t_tokens // TM_EXP + N_EXPERTS
    cnt = counts[0, N_GROUPS:N_GROUPS + N_EXPERTS].astype(I32)
    padded = (cnt + TM_EXP - 1) // TM_EXP * TM_EXP
    seg_end = jnp.cumsum(padded)
    seg_start = seg_end - padded
    n_used = seg_end[-1] // TM_EXP
    tile_id = jnp.arange(n_tiles, dtype=I32)
    tile_row = jnp.minimum(tile_id, n_used - 1) * TM_EXP
    tile_expert = jnp.sum((tile_row[:, None] >= seg_end[None, :]).astype(I32), axis=1)
    ids = jnp.arange(N_EXPERTS, dtype=I32)

    def slot_rows(k):
        start = jnp.sum(jnp.where(info[:, k][:, None] == ids, seg_start, 0), axis=-1)
        return start + info[:, 2 + k]

    def per_tile(n):
        return jnp.concatenate([p.reshape(t_tokens // n, 1, n) for p in pos], axis=2)

    pos = [slot_rows(0), slot_rows(1)]
    xs = _dispatch(per_tile(TD_DISP), x1, n_tiles * TM_EXP)
    first = jnp.concatenate([jnp.ones((1,), I32), (tile_expert[1:] != tile_expert[:-1]).astype(I32)])
    first = jnp.where(tile_id < n_used, first, 0)
    slot = (jnp.cumsum(first) - 1) % 2
    later = (ids[None, :] > ids[:, None]) & (cnt[None, :] > 0)
    next_of = jnp.min(jnp.where(later, ids[None, :], N_EXPERTS), axis=1)
    next_of = jnp.where(next_of < N_EXPERTS, next_of, -1)
    nxt = jnp.sum(jnp.where(tile_expert[:, None] == ids[None, :], next_of[None, :], 0), axis=1)
    ys = _experts(tile_expert, n_used.reshape(1), first, nxt, slot.astype(I32), xs, fg, wg, wu, wd, layer)
    return _combine(per_tile(TC_COMB), x1, wts, ys)


def _router_params(w_rg, b_rg, w_re, b_re):
    pad = LANES - N_GROUPS - N_EXPERTS
    wr = jnp.concatenate([w_rg, w_re, jnp.zeros((D_MODEL, pad), F32)], axis=1).astype(BF16)
    br = jnp.concatenate([b_rg, b_re, jnp.zeros((pad,), F32)]).reshape(1, LANES)
    return wr, br


def kernel(x, positions, even_norm, w_in, q_norm, k_norm, conv_w, conv_b, conv_ln_g, conv_ln_b, w_out,
           odd_norm, pool_w, pool_b, pool_scale, ffn_norm, w_router_group, b_router_group,
           w_router_expert, b_router_expert, w_expert_gate, w_expert_up, w_expert_down):
    batch, seq, d = x.shape
    depth = ffn_norm.shape[0]
    assert d == D_MODEL and seq % ATTN_CHUNK == 0 and seq % TM_OUT == 0
    t_tokens = batch * seq
    half = HEAD_DIM // 2
    inv_freq = jnp.float32(ROPE_THETA) ** (-jnp.arange(half, dtype=F32) / half)
    invf2 = jnp.concatenate([inv_freq, inv_freq]).reshape(1, HEAD_DIM)
    pos2 = positions.reshape(t_tokens, 1)
    row = lambda a: a.reshape(1, -1)

    x2 = x.reshape(t_tokens, d)
    for i in range(depth):
        j = i // 2
        wr, br = _router_params(w_router_group[i], b_router_group[i], w_router_expert[i], b_router_expert[i])
        fg = row(ffn_norm[i])
        if i % 2 == 0:
            q, k, v, u = _inproj(x2, pos2, row(even_norm[j]), w_in[j].astype(BF16), row(q_norm[j]),
                                 row(k_norm[j]), invf2, batch, seq)
            attn = _attention(q, k, v, batch, seq)
            conv = _conv(u, conv_w[j], row(conv_b[j]), row(conv_ln_g[j]), row(conv_ln_b[j]), batch, seq)
            x1, info, wts, counts = _outproj_router(attn, conv, x2, w_out[j].astype(BF16), fg, wr, br,
                                                    batch, seq)
        else:
            x1, info, wts, counts = _pool_router(x2, row(odd_norm[j]), pool_w[j].astype(BF16),
                                                 row(pool_b[j]), row(pool_scale[j]), fg, wr, br, batch, seq)
        x2 = _moe(x1, info, wts, counts, fg, w_expert_gate, w_expert_up, w_expert_down, i)
    return x2.reshape(batch, seq, d)
```
